```python
import math
import jax
import jax.numpy as jnp
from jax import lax
import numpy as np

D_MODEL = 1024
BATCH = 16
SEQ = 2048
DEPTH = 4

N_EVEN = (DEPTH + 1) // 2
N_ODD = DEPTH // 2
GRID_W = 64
CONV_DIM = 512
CONV_WIDTH = 31
SSM_DIM = 512
SSM_GROUP = 16
SSM_GROUPS = SSM_DIM // SSM_GROUP
SSM_STATE = 64
MIX_DIM = CONV_DIM + SSM_DIM
AB_IN_DIM = 2 * CONV_DIM + SSM_DIM
NA_HEADS = 16
NA_HEAD_DIM = D_MODEL // NA_HEADS
NA_WIN_ROWS = 8
NA_WIN_COLS = 16
FFN_DIM = -(-(8 * D_MODEL) // (3 * 256)) * 256
NORM_EPS = 1e-6
LN_EPS = 1e-5
MASK_VALUE = -1e30

kernel_name = 'hybrid_conv_s5_natten_encoder'


def rms_norm(x, gain):
    xf = x.astype(jnp.float32)
    y = xf * lax.rsqrt(jnp.mean(xf * xf, axis=-1, keepdims=True) + NORM_EPS)
    return (y * gain.astype(jnp.float32)).astype(x.dtype)


def layer_norm(x, gain, bias):
    xf = x.astype(jnp.float32)
    xc = xf - jnp.mean(xf, axis=-1, keepdims=True)
    var = jnp.mean(xc * xc, axis=-1, keepdims=True)
    y = xc * lax.rsqrt(var + LN_EPS) * gain.astype(jnp.float32) + bias.astype(jnp.float32)
    return y.astype(x.dtype)


def swiglu_ffn(h, w_gate, w_up, w_down):
    return (jax.nn.silu(h @ w_gate) * (h @ w_up)) @ w_down


def conformer_conv(za, zg, conv_w, conv_b, ln_g, ln_b):
    a = za * jax.nn.sigmoid(zg)
    a = lax.conv_general_dilated(
        a, conv_w[:, None, :].astype(a.dtype), window_strides=(1,),
        padding=[(CONV_WIDTH // 2, CONV_WIDTH // 2)],
        dimension_numbers=('NWC', 'WIO', 'NWC'),
        feature_group_count=CONV_DIM) + conv_b
    return jax.nn.silu(layer_norm(a, ln_g, ln_b))


def _complex_affine_combine(e1, e2):
    a1r, a1i, b1r, b1i = e1
    a2r, a2i, b2r, b2i = e2
    return (a2r * a1r - a2i * a1i,
            a2r * a1i + a2i * a1r,
            a2r * b1r - a2i * b1i + b2r,
            a2r * b1i + a2i * b1r + b2i)


def s5_direction(u, lam_re, lam_im, log_step, b_re, b_im, c_re, c_im):
    dt = jnp.exp(log_step.astype(jnp.float32))[:, None]
    lr = lam_re.astype(jnp.float32)
    li = lam_im.astype(jnp.float32)
    mag = jnp.exp(lr * dt)
    ar = mag * jnp.cos(li * dt)
    ai = mag * jnp.sin(li * dt)
    den = lr * lr + li * li
    pr = ar - 1.0
    coef_re = (pr * lr + ai * li) / den
    coef_im = (ai * lr - pr * li) / den
    bu_re = jnp.einsum('bsgc,gnc->bsgn', u, b_re.astype(jnp.float32))
    bu_im = jnp.einsum('bsgc,gnc->bsgn', u, b_im.astype(jnp.float32))
    x_re = coef_re * bu_re - coef_im * bu_im
    x_im = coef_re * bu_im + coef_im * bu_re
    seq = u.shape[1]
    a_re = jnp.broadcast_to(ar, (1, seq) + ar.shape)
    a_im = jnp.broadcast_to(ai, (1, seq) + ai.shape)
    _, _, s_re, s_im = lax.associative_scan(
        _complex_affine_combine, (a_re, a_im, x_re, x_im), axis=1)
    return (jnp.einsum('bsgn,gcn->bsgc', s_re, c_re.astype(jnp.float32))
            - jnp.einsum('bsgn,gcn->bsgc', s_im, c_im.astype(jnp.float32)))


def s5_mixer(u, lam_re, lam_im, log_step, b_re, b_im, c_re, c_im, d_skip, glu_w, glu_b):
    bsz, seq, _ = u.shape
    uf = u.astype(jnp.float32)
    ug = uf.reshape(bsz, seq, SSM_GROUPS, SSM_GROUP)
    y_fwd = s5_direction(ug, lam_re[0], lam_im[0], log_step[0], b_re[0], b_im[0], c_re[0], c_im[0])
    y_bwd = jnp.flip(s5_direction(jnp.flip(ug, axis=1), lam_re[1], lam_im[1], log_step[1],
                                  b_re[1], b_im[1], c_re[1], c_im[1]), axis=1)
    y = (y_fwd + y_bwd).reshape(bsz, seq, SSM_DIM) + d_skip.astype(jnp.float32) * uf
    y = jax.nn.gelu(y.astype(u.dtype))
    return y * jax.nn.sigmoid(y @ glu_w + glu_b)


def conv_ssm_layer(h, w_in, conv_w, conv_b, ln_g, ln_b, lam_re, lam_im, log_step,
                   b_re, b_im, c_re, c_im, d_skip, glu_w, glu_b, w_out):
    z = h @ w_in
    za = z[..., :CONV_DIM]
    zg = z[..., CONV_DIM:2 * CONV_DIM]
    u = z[..., 2 * CONV_DIM:]
    ya = conformer_conv(za, zg, conv_w, conv_b, ln_g, ln_b)
    yb = s5_mixer(u, lam_re, lam_im, log_step, b_re, b_im, c_re, c_im, d_skip, glu_w, glu_b)
    return jnp.concatenate([ya, yb], axis=-1) @ w_out


def neighbourhood_attention(h, w_qkv, q_gain, k_gain, rpb, w_out):
    bsz, seq, _ = h.shape
    rows = seq // GRID_W
    kr = min(NA_WIN_ROWS, rows)
    qkv = (h @ w_qkv).reshape(bsz, seq, 3, NA_HEADS, NA_HEAD_DIM)
    q = rms_norm(qkv[:, :, 0], q_gain)
    k = rms_norm(qkv[:, :, 1], k_gain)
    v = qkv[:, :, 2]

    def to_grid(t):
        return t.reshape(bsz, rows, GRID_W, NA_HEADS, NA_HEAD_DIM).transpose(0, 3, 1, 2, 4)

    q, k, v = to_grid(q), to_grid(k), to_grid(v)
    row_ids = jnp.arange(rows)
    row_start = jnp.clip(row_ids - kr // 2, 0, rows - kr)
    cols = jnp.arange(GRID_W)
    col_start = jnp.clip(cols - NA_WIN_COLS // 2, 0, GRID_W - NA_WIN_COLS)
    col_mask = ((cols[None, :] >= col_start[:, None])
                & (cols[None, :] < col_start[:, None] + NA_WIN_COLS))
    col_idx = jnp.clip(cols[None, :] - cols[:, None], 1 - NA_WIN_COLS, NA_WIN_COLS - 1) + (NA_WIN_COLS - 1)
    scale = NA_HEAD_DIM ** -0.5

    def attend_row(args):
        r, rs = args
        q_r = lax.dynamic_index_in_dim(q, r, axis=2, keepdims=False)
        k_blk = lax.dynamic_slice_in_dim(k, rs, kr, axis=2)
        v_blk = lax.dynamic_slice_in_dim(v, rs, kr, axis=2)
        s = jnp.einsum('bhqd,bhjkd->bhqjk', q_r, k_blk).astype(jnp.float32) * scale
        row_idx = rs + jnp.arange(kr) - r + (NA_WIN_ROWS - 1)
        bias = rpb[:, row_idx][:, :, col_idx]
        s = s + jnp.transpose(bias, (0, 2, 1, 3)).astype(jnp.float32)[None]
        s = jnp.where(col_mask[None, None, :, None, :], s, MASK_VALUE)
        p = jax.nn.softmax(s.reshape(bsz, NA_HEADS, GRID_W, kr * GRID_W), axis=-1).reshape(s.shape)
        return jnp.einsum('bhqjk,bhjkd->bhqd', p.astype(v_blk.dtype), v_blk)

    out = lax.map(attend_row, (row_ids, row_start))
    out = out.transpose(1, 0, 3, 2, 4).reshape(bsz, seq, D_MODEL)
    return out @ w_out


def setup_inputs(seed: int = 0) -> dict:
    key = jax.random.key(seed)
    ks = jax.random.split(key, 27)

    def nrm(k, shape, scale):
        return scale * jax.random.normal(k, shape, jnp.float32)

    lam_im_base = jnp.pi * jnp.arange(SSM_STATE, dtype=jnp.float32)
    return {
        'x': nrm(ks[0], (BATCH, SEQ, D_MODEL), 1.0),
        'mix_norm': 1.0 + nrm(ks[1], (DEPTH, D_MODEL), 0.02),
        'ffn_norm': 1.0 + nrm(ks[2], (DEPTH, D_MODEL), 0.02),
        'ffn_w_gate': nrm(ks[3], (DEPTH, D_MODEL, FFN_DIM), D_MODEL ** -0.5),
        'ffn_w_up': nrm(ks[4], (DEPTH, D_MODEL, FFN_DIM), D_MODEL ** -0.5),
        'ffn_w_down': nrm(ks[5], (DEPTH, FFN_DIM, D_MODEL), FFN_DIM ** -0.5),
        'ab_w_in': nrm(ks[6], (N_EVEN, D_MODEL, AB_IN_DIM), D_MODEL ** -0.5),
        'conv_w': nrm(ks[7], (N_EVEN, CONV_WIDTH, CONV_DIM), CONV_WIDTH ** -0.5),
        'conv_b': nrm(ks[8], (N_EVEN, CONV_DIM), 0.01),
        'conv_ln_g': 1.0 + nrm(ks[9], (N_EVEN, CONV_DIM), 0.02),
        'conv_ln_b': nrm(ks[10], (N_EVEN, CONV_DIM), 0.01),
        'ssm_lambda_re': -0.5 + nrm(ks[11], (N_EVEN, 2, SSM_GROUPS, SSM_STATE), 0.01),
        'ssm_lambda_im': lam_im_base + nrm(ks[12], (N_EVEN, 2, SSM_GROUPS, SSM_STATE), 0.01),
        'ssm_log_step': jax.random.uniform(ks[13], (N_EVEN, 2, SSM_GROUPS), jnp.float32,
                                           minval=math.log(1e-3), maxval=math.log(1e-1)),
        'ssm_b_re': nrm(ks[14], (N_EVEN, 2, SSM_GROUPS, SSM_STATE, SSM_GROUP), (2 * SSM_GROUP) ** -0.5),
        'ssm_b_im': nrm(ks[15], (N_EVEN, 2, SSM_GROUPS, SSM_STATE, SSM_GROUP), (2 * SSM_GROUP) ** -0.5),
        'ssm_c_re': nrm(ks[16], (N_EVEN, 2, SSM_GROUPS, SSM_GROUP, SSM_STATE), (2 * SSM_STATE) ** -0.5),
        'ssm_c_im': nrm(ks[17], (N_EVEN, 2, SSM_GROUPS, SSM_GROUP, SSM_STATE), (2 * SSM_STATE) ** -0.5),
        'ssm_d': nrm(ks[18], (N_EVEN, SSM_DIM), 0.5),
        'ssm_glu_w': nrm(ks[19], (N_EVEN, SSM_DIM, SSM_DIM), SSM_DIM ** -0.5),
        'ssm_glu_b': nrm(ks[20], (N_EVEN, SSM_DIM), 0.01),
        'ab_w_out': nrm(ks[21], (N_EVEN, MIX_DIM, D_MODEL), MIX_DIM ** -0.5),
        'na_w_qkv': nrm(ks[22], (N_ODD, D_MODEL, 3 * D_MODEL), D_MODEL ** -0.5),
        'na_q_norm': 1.0 + nrm(ks[23], (N_ODD, NA_HEAD_DIM), 0.02),
        'na_k_norm': 1.0 + nrm(ks[24], (N_ODD, NA_HEAD_DIM), 0.02),
        'na_rpb': nrm(ks[25], (N_ODD, NA_HEADS, 2 * NA_WIN_ROWS - 1, 2 * NA_WIN_COLS - 1), 0.1),
        'na_w_out': nrm(ks[26], (N_ODD, D_MODEL, D_MODEL), D_MODEL ** -0.5),
    }


def reference(x, mix_norm, ffn_norm, ffn_w_gate, ffn_w_up, ffn_w_down, ab_w_in, conv_w, conv_b,
              conv_ln_g, conv_ln_b, ssm_lambda_re, ssm_lambda_im, ssm_log_step, ssm_b_re, ssm_b_im,
              ssm_c_re, ssm_c_im, ssm_d, ssm_glu_w, ssm_glu_b, ab_w_out, na_w_qkv, na_q_norm,
              na_k_norm, na_rpb, na_w_out):
    for layer in range(DEPTH):
        i = layer // 2
        h = rms_norm(x, mix_norm[layer])
        if layer % 2 == 0:
            h = conv_ssm_layer(h, ab_w_in[i], conv_w[i], conv_b[i], conv_ln_g[i], conv_ln_b[i],
                               ssm_lambda_re[i], ssm_lambda_im[i], ssm_log_step[i],
                               ssm_b_re[i], ssm_b_im[i], ssm_c_re[i], ssm_c_im[i],
                               ssm_d[i], ssm_glu_w[i], ssm_glu_b[i], ab_w_out[i])
        else:
            h = neighbourhood_attention(h, na_w_qkv[i], na_q_norm[i], na_k_norm[i],
                                        na_rpb[i], na_w_out[i])
        x = x + h
        x = x + swiglu_ffn(rms_norm(x, ffn_norm[layer]), ffn_w_gate[layer],
                           ffn_w_up[layer], ffn_w_down[layer])
    return x
```

```python
import functools

import numpy as np
import jax
import jax.numpy as jnp
from jax import lax
from jax.experimental import pallas as pl
from jax.experimental.pallas import tpu as pltpu

F32 = jnp.float32
BF16 = jnp.bfloat16
HIGHEST = lax.Precision.HIGHEST

GRID_W = 64
CONV_DIM = 512
CONV_WIDTH = 31
SSM_DIM = 512
SSM_GROUP = 16
SSM_GROUPS = SSM_DIM // SSM_GROUP
SSM_STATE = 64
NA_HEADS = 16
NA_HEAD_DIM = 64
NA_WIN_ROWS = 8
NA_WIN_COLS = 16
NORM_EPS = 1e-6
LN_EPS = 1e-5
MASK_VALUE = -1e30

S5_CHUNK = 64
V7X_LANES = 128
V7X_VMEM_LIMIT = 56 * 1024 * 1024
ROW_TILE = 512
CONV_TILE = 32
CONV_PAD = 16


def _params(n_axes):
    return pltpu.CompilerParams(dimension_semantics=("arbitrary",) * n_axes,
                                vmem_limit_bytes=V7X_VMEM_LIMIT)


def _resident(shape):
    return pl.BlockSpec(shape, lambda *_: (0,) * len(shape), pipeline_mode=pl.Buffered(1))


def _rms_norm(x, gain):
    ms = jnp.mean(x * x, axis=-1, keepdims=True)
    return x * lax.rsqrt(ms + NORM_EPS) * gain


def _dot(a, b):
    return jnp.dot(a, b, preferred_element_type=F32)


def _dot_nt(a, b, precision=None):
    return lax.dot_general(a, b, (((1,), (1,)), ((), ())), precision=precision,
                           preferred_element_type=F32)


def _norm_matmul_body(x_ref, g_ref, w_ref, o_ref):
    h = _rms_norm(x_ref[...], g_ref[...]).astype(BF16)
    o_ref[...] = _dot(h, w_ref[...]).astype(o_ref.dtype)


def _norm_matmul(x, gain, w):
    m, d = x.shape
    n = w.shape[1]
    return pl.pallas_call(
        _norm_matmul_body,
        grid=(m // ROW_TILE,),
        in_specs=[pl.BlockSpec((ROW_TILE, d), lambda i: (i, 0)),
                  _resident((1, d)), _resident((d, n))],
        out_specs=pl.BlockSpec((ROW_TILE, n), lambda i: (i, 0)),
        out_shape=jax.ShapeDtypeStruct((m, n), BF16),
        compiler_params=_params(1),
        name="norm_matmul",
    )(x, gain.reshape(1, d), w)


def _ffn_tail(x1, fg_ref, wg_ref, wu_ref, wd_ref):
    h = _rms_norm(x1, fg_ref[...]).astype(BF16)
    g = _dot(h, wg_ref[...])
    u = _dot(h, wu_ref[...])
    a = (g * jax.nn.sigmoid(g) * u).astype(BF16)
    return x1 + _dot(a, wd_ref[...])


def _attn_ffn_body(x_ref, a_ref, wo_ref, fg_ref, wg_ref, wu_ref, wd_ref, o_ref):
    x1 = x_ref[...] + _dot(a_ref[...], wo_ref[...])
    o_ref[...] = _ffn_tail(x1, fg_ref, wg_ref, wu_ref, wd_ref)


def _ssm_ffn_body(x_ref, ya_ref, ys_ref, u_ref, d_ref, gw_ref, gb_ref, woa_ref, wob_ref,
                  fg_ref, wg_ref, wu_ref, wd_ref, o_ref):
    y = ys_ref[...].astype(F32) + d_ref[...] * u_ref[...].astype(F32)
    y = jax.nn.gelu(y, approximate=True)
    gate = jax.nn.sigmoid(_dot(y.astype(BF16), gw_ref[...]) + gb_ref[...])
    yb = (y * gate).astype(BF16)
    x1 = x_ref[...] + _dot(ya_ref[...], woa_ref[...]) + _dot(yb, wob_ref[...])
    o_ref[...] = _ffn_tail(x1, fg_ref, wg_ref, wu_ref, wd_ref)


def _row_spec(width, col_block=0):
    return pl.BlockSpec((ROW_TILE, width), lambda i: (i, col_block))


def _attn_ffn(x, attn, w_out, ffn_gain, w_gate, w_up, w_down):
    m, d = x.shape
    f = w_gate.shape[1]
    return pl.pallas_call(
        _attn_ffn_body,
        grid=(m // ROW_TILE,),
        in_specs=[_row_spec(d), _row_spec(d), _resident((d, d)), _resident((1, d)),
                  _resident((d, f)), _resident((d, f)), _resident((f, d))],
        out_specs=_row_spec(d),
        out_shape=jax.ShapeDtypeStruct((m, d), F32),
        compiler_params=_params(1),
        name="attn_out_ffn",
    )(x, attn, w_out, ffn_gain.reshape(1, d), w_gate, w_up, w_down)


def _ssm_ffn(x, ya, ys, z, d_skip, glu_w, glu_b, w_out, ffn_gain, w_gate, w_up, w_down):
    m, d = x.shape
    f = w_gate.shape[1]
    c = SSM_DIM
    return pl.pallas_call(
        _ssm_ffn_body,
        grid=(m // ROW_TILE,),
        in_specs=[_row_spec(d), _row_spec(CONV_DIM), _row_spec(c),
                  _row_spec(c, col_block=2 * CONV_DIM // c),
                  _resident((1, c)), _resident((c, c)), _resident((1, c)),
                  _resident((CONV_DIM, d)), _resident((c, d)), _resident((1, d)),
                  _resident((d, f)), _resident((d, f)), _resident((f, d))],
        out_specs=_row_spec(d),
        out_shape=jax.ShapeDtypeStruct((m, d), F32),
        compiler_params=_params(1),
        name="ssm_out_ffn",
    )(x, ya, ys, z, d_skip.reshape(1, c), glu_w, glu_b.reshape(1, c),
      w_out[:CONV_DIM], w_out[CONV_DIM:], ffn_gain.reshape(1, d), w_gate, w_up, w_down)


def _conv_body(za_ref, zg_ref, w_ref, cb_ref, lg_ref, lb_ref, o_ref, apad_ref, *, seq):
    c = CONV_DIM
    apad_ref[0:CONV_PAD, :] = jnp.zeros((CONV_PAD, c), F32)
    apad_ref[CONV_PAD + seq:, :] = jnp.zeros((CONV_PAD, c), F32)

    glu_tile = 256

    def glu_step(i, carry):
        r0 = pl.multiple_of(i * glu_tile, glu_tile)
        za = za_ref[0, pl.ds(r0, glu_tile), :].astype(F32)
        zg = zg_ref[0, pl.ds(r0, glu_tile), :].astype(F32)
        apad_ref[pl.ds(CONV_PAD + r0, glu_tile), :] = za * jax.nn.sigmoid(zg)
        return carry

    lax.fori_loop(0, seq // glu_tile, glu_step, 0)

    first = CONV_PAD - CONV_WIDTH // 2

    def conv_step(i, carry):
        r0 = pl.multiple_of(i * CONV_TILE, CONV_TILE)
        win = apad_ref[pl.ds(r0, CONV_TILE + 2 * CONV_PAD), :]
        acc = jnp.zeros((CONV_TILE, c), F32)
        for k in range(CONV_WIDTH):
            acc = acc + w_ref[k:k + 1, :] * win[first + k:first + k + CONV_TILE]
        acc = acc + cb_ref[...]
        mu = jnp.mean(acc, axis=-1, keepdims=True)
        xc = acc - mu
        var = jnp.mean(xc * xc, axis=-1, keepdims=True)
        y = xc * lax.rsqrt(var + LN_EPS) * lg_ref[...] + lb_ref[...]
        o_ref[0, pl.ds(r0, CONV_TILE), :] = (y * jax.nn.sigmoid(y)).astype(o_ref.dtype)
        return carry

    lax.fori_loop(0, seq // CONV_TILE, conv_step, 0)


def _conv_module(z3, conv_w, conv_b, ln_g, ln_b):
    bsz, seq, _ = z3.shape
    c = CONV_DIM
    return pl.pallas_call(
        functools.partial(_conv_body, seq=seq),
        grid=(bsz,),
        in_specs=[pl.BlockSpec((1, seq, c), lambda b: (b, 0, 0)),
                  pl.BlockSpec((1, seq, c), lambda b: (b, 0, 1)),
                  _resident((CONV_WIDTH, c)), _resident((1, c)), _resident((1, c)),
                  _resident((1, c))],
        out_specs=pl.BlockSpec((1, seq, c), lambda b: (b, 0, 0)),
        out_shape=jax.ShapeDtypeStruct((bsz, seq, c), BF16),
        scratch_shapes=[pltpu.VMEM((seq + 2 * CONV_PAD, c), F32)],
        compiler_params=_params(1),
        name="conformer_conv",
    )(z3, z3, conv_w, conv_b.reshape(1, c), ln_g.reshape(1, c), ln_b.reshape(1, c))


def _s5_body(x_ref, par_ref, btr_ref, bti_ref, cr_ref, ci_ref, y_ref, t_ref, w8_ref,
             *, chunk, n_chunks, bsz):
    n = SSM_STATE
    p = SSM_GROUP
    r = p * chunk
    step = lax.broadcasted_iota(jnp.int32, (chunk, 1), 0).astype(F32)
    rep = (lax.broadcasted_iota(jnp.int32, (r, chunk), 0) // p
           == lax.broadcasted_iota(jnp.int32, (r, chunk), 1)).astype(F32)

    def expand(a):
        return jnp.dot(rep, a, precision=HIGHEST, preferred_element_type=F32)

    def tile_rows(a):
        return jnp.concatenate([a] * chunk, axis=0)

    wc = jnp.zeros((p, 2 * r), F32)
    m_ops, q_ops, a_chunk = [], [], []
    for d in range(2):
        lr = par_ref[0, d, 0:1, :]
        li = par_ref[0, d, 1:2, :]
        dt = jnp.exp(par_ref[0, d, 2:3, :])
        lr_dt = lr * dt
        li_dt = li * dt

        def cpow(e):
            mag = jnp.exp(lr_dt * e)
            ang = li_dt * e
            return mag * jnp.cos(ang), mag * jnp.sin(ang)

        ar, ai = cpow(1.0)
        den = lr * lr + li * li
        pr = ar - 1.0
        coef_re = (pr * lr + ai * li) / den
        coef_im = (ai * lr - pr * li) / den
        btr = btr_ref[0, d]
        bti = bti_ref[0, d]
        bbr = tile_rows(coef_re * btr - coef_im * bti)
        bbi = tile_rows(coef_re * bti + coef_im * btr)
        ctr = tile_rows(cr_ref[0, d])
        cti = tile_rows(ci_ref[0, d])
        if d == 0:
            e_m, e_q = (chunk - 1.0) - step, step + 1.0
        else:
            e_m, e_q = step, chunk - step
        pmr, pmi = (expand(v) for v in cpow(e_m))
        pqr, pqi = (expand(v) for v in cpow(e_q))
        mr = bbr * pmr - bbi * pmi
        mi = bbr * pmi + bbi * pmr
        qr = ctr * pqr - cti * pqi
        qi = ctr * pqi + cti * pqr
        m_ops.append((mr.astype(BF16), mi.astype(BF16)))
        q_ops.append((qr.astype(BF16), qi.astype(BF16)))
        a_chunk.append(cpow(float(chunk)))

        if d == 0:
            pad = lambda v: jnp.concatenate([v, jnp.zeros((r, n), F32)], axis=0)
        else:
            pad = lambda v: jnp.concatenate(
                [jnp.zeros((r - p, n), F32), v, jnp.zeros((p, n), F32)], axis=0)
        wc = wc + _dot_nt(cr_ref[0, d], pad(mr), HIGHEST) - _dot_nt(ci_ref[0, d], pad(mi), HIGHEST)

    shifts = V7X_LANES // p
    width = 2 * r - V7X_LANES
    for s in range(shifts):
        w8_ref[s] = wc[:, p * s:p * s + width]
    for t in range(chunk):
        a, s = divmod(p * (chunk - 1 - t), V7X_LANES)
        t_ref[p * t:p * (t + 1), :] = w8_ref[s // p, :, V7X_LANES * a:V7X_LANES * a + r].astype(BF16)

    x = x_ref[0]
    y = _dot_nt(x, t_ref[...])

    for d in range(2):
        mr, mi = m_ops[d]
        er = _dot(x, mr)
        ei = _dot(x, mi)
        alr, ali = a_chunk[d]
        sr = jnp.zeros((bsz, n), F32)
        si = jnp.zeros((bsz, n), F32)
        prev_r, prev_i = [None] * n_chunks, [None] * n_chunks
        order = range(n_chunks) if d == 0 else range(n_chunks - 1, -1, -1)
        for j in order:
            prev_r[j], prev_i[j] = sr, si
            ejr = er[j * bsz:(j + 1) * bsz]
            eji = ei[j * bsz:(j + 1) * bsz]
            sr, si = alr * sr - ali * si + ejr, alr * si + ali * sr + eji
        qr, qi = q_ops[d]
        y = y + _dot_nt(jnp.concatenate(prev_r, axis=0).astype(BF16), qr)
        y = y - _dot_nt(jnp.concatenate(prev_i, axis=0).astype(BF16), qi)

    y_ref[0] = y.astype(y_ref.dtype)


def _s5_scan(u3, lam_re, lam_im, log_step, b_re, b_im, c_re, c_im):
    bsz, seq, _ = u3.shape
    g, p, n, chunk = SSM_GROUPS, SSM_GROUP, SSM_STATE, S5_CHUNK
    assert seq % chunk == 0 and bsz % 8 == 0
    n_chunks = seq // chunk
    r = p * chunk
    rows = n_chunks * bsz
    x = u3.reshape(bsz, n_chunks, chunk, g, p).transpose(3, 1, 0, 2, 4).reshape(g, rows, r)
    par = jnp.stack([lam_re, lam_im, jnp.broadcast_to(log_step[..., None], lam_re.shape)], axis=2)
    par = par.transpose(1, 0, 2, 3)
    bt_re = b_re.transpose(1, 0, 3, 2)
    bt_im = b_im.transpose(1, 0, 3, 2)
    cg_re = c_re.transpose(1, 0, 2, 3)
    cg_im = c_im.transpose(1, 0, 2, 3)
    grp = lambda *tail: pl.BlockSpec((1,) + tail, lambda i: (i,) + (0,) * len(tail))
    y = pl.pallas_call(
        functools.partial(_s5_body, chunk=chunk, n_chunks=n_chunks, bsz=bsz),
        grid=(g,),
        in_specs=[grp(rows, r), grp(2, 3, n), grp(2, p, n), grp(2, p, n), grp(2, p, n),
                  grp(2, p, n)],
        out_specs=grp(rows, r),
        out_shape=jax.ShapeDtypeStruct((g, rows, r), BF16),
        scratch_shapes=[pltpu.VMEM((r, r), BF16),
                        pltpu.VMEM((V7X_LANES // p, p, 2 * r - V7X_LANES), F32)],
        compiler_params=_params(1),
        name="s5_chunked",
    )(x, par, bt_re, bt_im, cg_re, cg_im)
    return y.reshape(g, n_chunks, bsz, chunk, p).transpose(2, 1, 3, 0, 4).reshape(bsz, seq, g * p)


def _na_body(q_ref, k_ref, v_ref, qg_ref, kg_ref, tab_ref, mask_ref, o_ref,
             q0_ref, q1_ref, kn_ref, *, rows):
    w = GRID_W
    keys = NA_WIN_ROWS * w
    head0 = lax.broadcasted_iota(jnp.int32, (1, 2 * NA_HEAD_DIM), 1) < NA_HEAD_DIM

    def head_norm(t, gain):
        sq = t * t
        s0 = jnp.sum(jnp.where(head0, sq, 0.0), axis=-1, keepdims=True) / NA_HEAD_DIM
        s1 = jnp.sum(jnp.where(head0, 0.0, sq), axis=-1, keepdims=True) / NA_HEAD_DIM
        inv = jnp.where(head0, lax.rsqrt(s0 + NORM_EPS), lax.rsqrt(s1 + NORM_EPS))
        return t * inv * gain

    qn = head_norm(q_ref[0].astype(F32), qg_ref[...]) * (NA_HEAD_DIM ** -0.5)
    q0_ref[...] = jnp.where(head0, qn, 0.0).astype(BF16)
    q1_ref[...] = jnp.where(head0, 0.0, qn).astype(BF16)
    kn_ref[...] = head_norm(k_ref[0].astype(F32), kg_ref[...]).astype(BF16)
    valid = mask_ref[...] != 0

    def row_step(row, carry):
        start = jnp.clip(row - NA_WIN_ROWS // 2, 0, rows - NA_WIN_ROWS)
        q_off = pl.multiple_of(row * w, w)
        k_off = pl.multiple_of(start * w, w)
        bias_row = start - row + (NA_WIN_ROWS - 1)
        kblk = kn_ref[pl.ds(k_off, keys), :]
        vblk = v_ref[0, pl.ds(k_off, keys), :]
        outs = []
        for h, qh_ref in enumerate((q0_ref, q1_ref)):
            s = _dot_nt(qh_ref[pl.ds(q_off, w), :], kblk) + tab_ref[0, h, bias_row]
            s = jnp.where(valid, s, MASK_VALUE)
            e = jnp.exp(s - jnp.max(s, axis=-1, keepdims=True))
            outs.append(_dot(e.astype(BF16), vblk) / jnp.sum(e, axis=-1, keepdims=True))
        o_ref[0, pl.ds(q_off, w), :] = jnp.where(head0, outs[0], outs[1]).astype(o_ref.dtype)
        return carry

    lax.fori_loop(0, rows, row_step, 0)


def _na_bias_tables(rpb):
    w = GRID_W
    cols = np.arange(w)
    col_idx = np.clip(cols[None, :] - cols[:, None], 1 - NA_WIN_COLS, NA_WIN_COLS - 1) + (NA_WIN_COLS - 1)
    cb = rpb[:, :, col_idx]
    idx = np.arange(NA_WIN_ROWS)[:, None] + np.arange(NA_WIN_ROWS)[None, :]
    t = cb[:, idx]
    t = t.transpose(0, 1, 3, 2, 4).reshape(NA_HEADS // 2, 2, NA_WIN_ROWS, w, NA_WIN_ROWS * w)
    return t


def _na_col_mask():
    w = GRID_W
    cols = np.arange(w)
    col_start = np.clip(cols - NA_WIN_COLS // 2, 0, w - NA_WIN_COLS)
    m = (cols[None, :] >= col_start[:, None]) & (cols[None, :] < col_start[:, None] + NA_WIN_COLS)
    return jnp.asarray(np.tile(m, (1, NA_WIN_ROWS)).astype(np.int32))


def _neighbourhood_attention(qkv3, q_gain, k_gain, rpb):
    bsz, seq, d3 = qkv3.shape
    d = d3 // 3
    rows = seq // GRID_W
    assert rows >= NA_WIN_ROWS and d == NA_HEADS * NA_HEAD_DIM
    pair = 2 * NA_HEAD_DIM
    n_pairs = d // pair
    keys = NA_WIN_ROWS * GRID_W
    blk = lambda off: pl.BlockSpec((1, seq, pair), lambda hp, b: (b, 0, off + hp))
    gains = lambda gvec: jnp.tile(gvec, 2).reshape(1, pair)
    return pl.pallas_call(
        functools.partial(_na_body, rows=rows),
        grid=(n_pairs, bsz),
        in_specs=[blk(0), blk(n_pairs), blk(2 * n_pairs), _resident((1, pair)), _resident((1, pair)),
                  pl.BlockSpec((1, 2, NA_WIN_ROWS, GRID_W, keys), lambda hp, b: (hp, 0, 0, 0, 0)),
                  _resident((GRID_W, keys))],
        out_specs=pl.BlockSpec((1, seq, pair), lambda hp, b: (b, 0, hp)),
        out_shape=jax.ShapeDtypeStruct((bsz, seq, d), BF16),
        scratch_shapes=[pltpu.VMEM((seq, pair), BF16)] * 3,
        compiler_params=_params(2),
        name="neighbourhood_attention",
    )(qkv3, qkv3, qkv3, gains(q_gain), gains(k_gain), _na_bias_tables(rpb), _na_col_mask())


def kernel(x, mix_norm, ffn_norm, ffn_w_gate, ffn_w_up, ffn_w_down, ab_w_in, conv_w, conv_b, conv_ln_g, conv_ln_b, ssm_lambda_re, ssm_lambda_im, ssm_log_step, ssm_b_re, ssm_b_im, ssm_c_re, ssm_c_im, ssm_d, ssm_glu_w, ssm_glu_b, ab_w_out, na_w_qkv, na_q_norm, na_k_norm, na_rpb, na_w_out):
    bsz, seq, d = x.shape
    m = bsz * seq
    assert m % ROW_TILE == 0
    depth = mix_norm.shape[0]
    bf = lambda w: w.astype(BF16)
    xs = x.reshape(m, d)
    for layer in range(depth):
        i = layer // 2
        ffn = (ffn_norm[layer], bf(ffn_w_gate[layer]), bf(ffn_w_up[layer]), bf(ffn_w_down[layer]))
        if layer % 2 == 0:
            z = _norm_matmul(xs, mix_norm[layer], bf(ab_w_in[i]))
            z3 = z.reshape(bsz, seq, -1)
            ya = _conv_module(z3, conv_w[i], conv_b[i], conv_ln_g[i], conv_ln_b[i])
            ys = _s5_scan(z3[..., 2 * CONV_DIM:], ssm_lambda_re[i], ssm_lambda_im[i],
                          ssm_log_step[i], ssm_b_re[i], ssm_b_im[i], ssm_c_re[i], ssm_c_im[i])
            xs = _ssm_ffn(xs, ya.reshape(m, -1), ys.reshape(m, -1), z, ssm_d[i], bf(ssm_glu_w[i]),
                          ssm_glu_b[i], bf(ab_w_out[i]), *ffn)
        else:
            qkv = _norm_matmul(xs, mix_norm[layer], bf(na_w_qkv[i]))
            attn = _neighbourhood_attention(qkv.reshape(bsz, seq, -1), na_q_norm[i], na_k_norm[i],
                                            na_rpb[i])
            xs = _attn_ffn(xs, attn.reshape(m, d), bf(na_w_out[i]), *ffn)
    return xs.reshape(bsz, seq, d)
```

```python
import functools

import numpy as np
import jax
import jax.numpy as jnp
from jax import lax
from jax.experimental import pallas as pl
from jax.experimental.pallas import tpu as pltpu

F32 = jnp.float32
BF16 = jnp.bfloat16
HIGHEST = lax.Precision.HIGHEST

GRID_W = 64
CONV_DIM = 512
CONV_WIDTH = 31
SSM_DIM = 512
SSM_GROUP = 16
SSM_GROUPS = SSM_DIM // SSM_GROUP
SSM_STATE = 64
NA_HEADS = 16
NA_HEAD_DIM = 64
NA_WIN_ROWS = 8
NA_WIN_COLS = 16
NORM_EPS = 1e-6
LN_EPS = 1e-5
MASK_VALUE = -1e30

S5_CHUNK = 64
V7X_LANES = 128
V7X_VMEM_LIMIT = 56 * 1024 * 1024
ROW_TILE = 512
CONV_TILE = 64
CONV_PAD = 16
V7X_SUBLANES = 8
NA_ROWS_PER_STEP = 4


def _params(n_axes):
    return pltpu.CompilerParams(dimension_semantics=("arbitrary",) * n_axes,
                                vmem_limit_bytes=V7X_VMEM_LIMIT)


def _resident(shape):
    return pl.BlockSpec(shape, lambda *_: (0,) * len(shape), pipeline_mode=pl.Buffered(1))


def _rms_norm(x, gain):
    ms = jnp.mean(x * x, axis=-1, keepdims=True)
    return x * lax.rsqrt(ms + NORM_EPS) * gain


def _dot(a, b):
    return jnp.dot(a, b, preferred_element_type=F32)


def _dot_nt(a, b, precision=None):
    return lax.dot_general(a, b, (((1,), (1,)), ((), ())), precision=precision,
                           preferred_element_type=F32)


def _norm_matmul_body(x_ref, g_ref, w_ref, o_ref):
    h = _rms_norm(x_ref[...], g_ref[...]).astype(BF16)
    o_ref[...] = _dot(h, w_ref[...]).astype(o_ref.dtype)


def _norm_matmul(x, gain, w):
    m, d = x.shape
    n = w.shape[1]
    return pl.pallas_call(
        _norm_matmul_body,
        grid=(m // ROW_TILE,),
        in_specs=[pl.BlockSpec((ROW_TILE, d), lambda i: (i, 0)),
                  _resident((1, d)), _resident((d, n))],
        out_specs=pl.BlockSpec((ROW_TILE, n), lambda i: (i, 0)),
        out_shape=jax.ShapeDtypeStruct((m, n), BF16),
        compiler_params=_params(1),
        name="norm_matmul",
    )(x, gain.reshape(1, d), w)


def _ffn_tail(x1, fg_ref, wg_ref, wu_ref, wd_ref):
    h = _rms_norm(x1, fg_ref[...]).astype(BF16)
    g = _dot(h, wg_ref[...])
    u = _dot(h, wu_ref[...])
    a = (g * jax.nn.sigmoid(g) * u).astype(BF16)
    return x1 + _dot(a, wd_ref[...])


def _attn_ffn_body(x_ref, a_ref, wo_ref, fg_ref, wg_ref, wu_ref, wd_ref, o_ref):
    x1 = x_ref[...] + _dot(a_ref[...], wo_ref[...])
    o_ref[...] = _ffn_tail(x1, fg_ref, wg_ref, wu_ref, wd_ref)


def _ssm_ffn_body(x_ref, ya_ref, ys_ref, u_ref, d_ref, gw_ref, gb_ref, woa_ref, wob_ref,
                  fg_ref, wg_ref, wu_ref, wd_ref, o_ref):
    y = ys_ref[...].astype(F32) + d_ref[...] * u_ref[...].astype(F32)
    y = jax.nn.gelu(y, approximate=True)
    gate = jax.nn.sigmoid(_dot(y.astype(BF16), gw_ref[...]) + gb_ref[...])
    yb = (y * gate).astype(BF16)
    x1 = x_ref[...] + _dot(ya_ref[...], woa_ref[...]) + _dot(yb, wob_ref[...])
    o_ref[...] = _ffn_tail(x1, fg_ref, wg_ref, wu_ref, wd_ref)


def _row_spec(width, col_block=0):
    return pl.BlockSpec((ROW_TILE, width), lambda i: (i, col_block))


def _attn_ffn(x, attn, w_out, ffn_gain, w_gate, w_up, w_down):
    m, d = x.shape
    f = w_gate.shape[1]
    return pl.pallas_call(
        _attn_ffn_body,
        grid=(m // ROW_TILE,),
        in_specs=[_row_spec(d), _row_spec(d), _resident((d, d)), _resident((1, d)),
                  _resident((d, f)), _resident((d, f)), _resident((f, d))],
        out_specs=_row_spec(d),
        out_shape=jax.ShapeDtypeStruct((m, d), F32),
        compiler_params=_params(1),
        name="attn_out_ffn",
    )(x, attn, w_out, ffn_gain.reshape(1, d), w_gate, w_up, w_down)


def _ssm_ffn(x, ya, ys, z, d_skip, glu_w, glu_b, w_out, ffn_gain, w_gate, w_up, w_down):
    m, d = x.shape
    f = w_gate.shape[1]
    c = SSM_DIM
    return pl.pallas_call(
        _ssm_ffn_body,
        grid=(m // ROW_TILE,),
        in_specs=[_row_spec(d), _row_spec(CONV_DIM), _row_spec(c),
                  _row_spec(c, col_block=2 * CONV_DIM // c),
                  _resident((1, c)), _resident((c, c)), _resident((1, c)),
                  _resident((CONV_DIM, d)), _resident((c, d)), _resident((1, d)),
                  _resident((d, f)), _resident((d, f)), _resident((f, d))],
        out_specs=_row_spec(d),
        out_shape=jax.ShapeDtypeStruct((m, d), F32),
        compiler_params=_params(1),
        name="ssm_out_ffn",
    )(x, ya, ys, z, d_skip.reshape(1, c), glu_w, glu_b.reshape(1, c),
      w_out[:CONV_DIM], w_out[CONV_DIM:], ffn_gain.reshape(1, d), w_gate, w_up, w_down)


def _conv_body(za_ref, zg_ref, w_ref, cb_ref, lg_ref, lb_ref, o_ref, apad_ref, *, seq):
    c = CONV_DIM
    apad_ref[0:CONV_PAD, :] = jnp.zeros((CONV_PAD, c), F32)
    apad_ref[CONV_PAD + seq:, :] = jnp.zeros((CONV_PAD, c), F32)

    glu_tile = 256

    def glu_step(i, carry):
        r0 = pl.multiple_of(i * glu_tile, glu_tile)
        za = za_ref[0, pl.ds(r0, glu_tile), :].astype(F32)
        zg = zg_ref[0, pl.ds(r0, glu_tile), :].astype(F32)
        apad_ref[pl.ds(CONV_PAD + r0, glu_tile), :] = za * jax.nn.sigmoid(zg)
        return carry

    lax.fori_loop(0, seq // glu_tile, glu_step, 0)

    first = CONV_PAD - CONV_WIDTH // 2
    sub = V7X_SUBLANES

    def conv_step(i, carry):
        r0 = pl.multiple_of(i * CONV_TILE, CONV_TILE)
        win = apad_ref[pl.ds(r0, CONV_TILE + 2 * CONV_PAD), :]
        acc = cb_ref[...]
        for r in range(sub):
            part = None
            for k in range(r, CONV_WIDTH, sub):
                term = w_ref[k:k + 1, :] * win[k - r:k - r + CONV_TILE + sub]
                part = term if part is None else part + term
            acc = acc + part[first + r:first + r + CONV_TILE]
        mu = jnp.mean(acc, axis=-1, keepdims=True)
        xc = acc - mu
        var = jnp.mean(xc * xc, axis=-1, keepdims=True)
        y = xc * lax.rsqrt(var + LN_EPS) * lg_ref[...] + lb_ref[...]
        o_ref[0, pl.ds(r0, CONV_TILE), :] = (y * jax.nn.sigmoid(y)).astype(o_ref.dtype)
        return carry

    lax.fori_loop(0, seq // CONV_TILE, conv_step, 0)


def _conv_module(z3, conv_w, conv_b, ln_g, ln_b):
    bsz, seq, _ = z3.shape
    c = CONV_DIM
    return pl.pallas_call(
        functools.partial(_conv_body, seq=seq),
        grid=(bsz,),
        in_specs=[pl.BlockSpec((1, seq, c), lambda b: (b, 0, 0)),
                  pl.BlockSpec((1, seq, c), lambda b: (b, 0, 1)),
                  _resident((CONV_WIDTH, c)), _resident((1, c)), _resident((1, c)),
                  _resident((1, c))],
        out_specs=pl.BlockSpec((1, seq, c), lambda b: (b, 0, 0)),
        out_shape=jax.ShapeDtypeStruct((bsz, seq, c), BF16),
        scratch_shapes=[pltpu.VMEM((seq + 2 * CONV_PAD, c), F32)],
        compiler_params=_params(1),
        name="conformer_conv",
    )(z3, z3, conv_w, conv_b.reshape(1, c), ln_g.reshape(1, c), ln_b.reshape(1, c))


def _s5_body(x_ref, par_ref, btr_ref, bti_ref, cr_ref, ci_ref, y_ref, t_ref, w8_ref,
             *, chunk, n_chunks, bsz):
    n = SSM_STATE
    p = SSM_GROUP
    r = p * chunk
    step = lax.broadcasted_iota(jnp.int32, (chunk, 1), 0).astype(F32)
    rep = (lax.broadcasted_iota(jnp.int32, (r, chunk), 0) // p
           == lax.broadcasted_iota(jnp.int32, (r, chunk), 1)).astype(F32)

    def expand(a):
        return jnp.dot(rep, a, precision=HIGHEST, preferred_element_type=F32)

    def tile_rows(a):
        return jnp.concatenate([a] * chunk, axis=0)

    wc = jnp.zeros((p, 2 * r), F32)
    m_ops, q_ops, a_chunk = [], [], []
    for d in range(2):
        lr = par_ref[0, d, 0:1, :]
        li = par_ref[0, d, 1:2, :]
        dt = jnp.exp(par_ref[0, d, 2:3, :])
        lr_dt = lr * dt
        li_dt = li * dt

        def cpow(e):
            mag = jnp.exp(lr_dt * e)
            ang = li_dt * e
            return mag * jnp.cos(ang), mag * jnp.sin(ang)

        ar, ai = cpow(1.0)
        den = lr * lr + li * li
        pr = ar - 1.0
        coef_re = (pr * lr + ai * li) / den
        coef_im = (ai * lr - pr * li) / den
        btr = btr_ref[0, d]
        bti = bti_ref[0, d]
        bbr = tile_rows(coef_re * btr - coef_im * bti)
        bbi = tile_rows(coef_re * bti + coef_im * btr)
        ctr = tile_rows(cr_ref[0, d])
        cti = tile_rows(ci_ref[0, d])
        if d == 0:
            e_m, e_q = (chunk - 1.0) - step, step + 1.0
        else:
            e_m, e_q = step, chunk - step
        pmr, pmi = (expand(v) for v in cpow(e_m))
        pqr, pqi = (expand(v) for v in cpow(e_q))
        mr = bbr * pmr - bbi * pmi
        mi = bbr * pmi + bbi * pmr
        qr = ctr * pqr - cti * pqi
        qi = ctr * pqi + cti * pqr
        m_ops.append((mr.astype(BF16), mi.astype(BF16)))
        q_ops.append((qr.astype(BF16), qi.astype(BF16)))
        a_chunk.append(cpow(float(chunk)))

        if d == 0:
            pad = lambda v: jnp.concatenate([v, jnp.zeros((r, n), F32)], axis=0)
        else:
            pad = lambda v: jnp.concatenate(
                [jnp.zeros((r - p, n), F32), v, jnp.zeros((p, n), F32)], axis=0)
        wc = wc + _dot_nt(cr_ref[0, d], pad(mr), HIGHEST) - _dot_nt(ci_ref[0, d], pad(mi), HIGHEST)

    shifts = V7X_LANES // p
    width = 2 * r - V7X_LANES
    for s in range(shifts):
        w8_ref[s] = wc[:, p * s:p * s + width]
    for t in range(chunk):
        a, s = divmod(p * (chunk - 1 - t), V7X_LANES)
        t_ref[p * t:p * (t + 1), :] = w8_ref[s // p, :, V7X_LANES * a:V7X_LANES * a + r].astype(BF16)

    x = x_ref[0]
    y = _dot_nt(x, t_ref[...])

    for d in range(2):
        mr, mi = m_ops[d]
        er = _dot(x, mr)
        ei = _dot(x, mi)
        alr, ali = a_chunk[d]
        sr = jnp.zeros((bsz, n), F32)
        si = jnp.zeros((bsz, n), F32)
        prev_r, prev_i = [None] * n_chunks, [None] * n_chunks
        order = range(n_chunks) if d == 0 else range(n_chunks - 1, -1, -1)
        for j in order:
            prev_r[j], prev_i[j] = sr, si
            ejr = er[j * bsz:(j + 1) * bsz]
            eji = ei[j * bsz:(j + 1) * bsz]
            sr, si = alr * sr - ali * si + ejr, alr * si + ali * sr + eji
        qr, qi = q_ops[d]
        y = y + _dot_nt(jnp.concatenate(prev_r, axis=0).astype(BF16), qr)
        y = y - _dot_nt(jnp.concatenate(prev_i, axis=0).astype(BF16), qi)

    y_ref[0] = y.astype(y_ref.dtype)


def _s5_scan(u3, lam_re, lam_im, log_step, b_re, b_im, c_re, c_im):
    bsz, seq, _ = u3.shape
    g, p, n, chunk = SSM_GROUPS, SSM_GROUP, SSM_STATE, S5_CHUNK
    assert seq % chunk == 0 and bsz % 8 == 0
    n_chunks = seq // chunk
    r = p * chunk
    rows = n_chunks * bsz
    x = u3.reshape(bsz, n_chunks, chunk, g, p).transpose(3, 1, 0, 2, 4).reshape(g, rows, r)
    par = jnp.stack([lam_re, lam_im, jnp.broadcast_to(log_step[..., None], lam_re.shape)], axis=2)
    par = par.transpose(1, 0, 2, 3)
    bt_re = b_re.transpose(1, 0, 3, 2)
    bt_im = b_im.transpose(1, 0, 3, 2)
    cg_re = c_re.transpose(1, 0, 2, 3)
    cg_im = c_im.transpose(1, 0, 2, 3)
    grp = lambda *tail: pl.BlockSpec((1,) + tail, lambda i: (i,) + (0,) * len(tail))
    y = pl.pallas_call(
        functools.partial(_s5_body, chunk=chunk, n_chunks=n_chunks, bsz=bsz),
        grid=(g,),
        in_specs=[grp(rows, r), grp(2, 3, n), grp(2, p, n), grp(2, p, n), grp(2, p, n),
                  grp(2, p, n)],
        out_specs=grp(rows, r),
        out_shape=jax.ShapeDtypeStruct((g, rows, r), BF16),
        scratch_shapes=[pltpu.VMEM((r, r), BF16),
                        pltpu.VMEM((V7X_LANES // p, p, 2 * r - V7X_LANES), F32)],
        compiler_params=_params(1),
        name="s5_chunked",
    )(x, par, bt_re, bt_im, cg_re, cg_im)
    return y.reshape(g, n_chunks, bsz, chunk, p).transpose(2, 1, 3, 0, 4).reshape(bsz, seq, g * p)


def _na_body(q_ref, k_ref, v_ref, qg_ref, kg_ref, tab_ref, mask_ref, o_ref,
             q0_ref, q1_ref, kn_ref, sa_ref, sb_ref, *, rows):
    w = GRID_W
    keys = NA_WIN_ROWS * w
    pair = 2 * NA_HEAD_DIM
    head0 = lax.broadcasted_iota(jnp.int32, (1, pair), 1) < NA_HEAD_DIM
    same_head = (lax.broadcasted_iota(jnp.int32, (pair, pair), 0) // NA_HEAD_DIM
                 == lax.broadcasted_iota(jnp.int32, (pair, pair), 1) // NA_HEAD_DIM).astype(BF16)

    def head_norm(t, gain):
        sq = t * t
        hi = sq.astype(BF16)
        lo = (sq - hi.astype(F32)).astype(BF16)
        ms = (_dot(hi, same_head) + _dot(lo, same_head)) / NA_HEAD_DIM
        return t * lax.rsqrt(ms + NORM_EPS) * gain

    qn = head_norm(q_ref[0].astype(F32), qg_ref[...]) * (NA_HEAD_DIM ** -0.5)
    q0_ref[...] = jnp.where(head0, qn, 0.0).astype(BF16)
    q1_ref[...] = jnp.where(head0, 0.0, qn).astype(BF16)
    kn_ref[...] = head_norm(k_ref[0].astype(F32), kg_ref[...]).astype(BF16)

    n_steps = rows // NA_ROWS_PER_STEP

    def offsets(step, j):
        row = step * NA_ROWS_PER_STEP + j
        start = jnp.clip(row - NA_WIN_ROWS // 2, 0, rows - NA_WIN_ROWS)
        return (pl.multiple_of(row * w, w), pl.multiple_of(start * w, w),
                start - row + (NA_WIN_ROWS - 1))

    def score_rows(step, stage_ref):
        for j in range(NA_ROWS_PER_STEP):
            q_off, k_off, _ = offsets(step, j)
            kblk = kn_ref[pl.ds(k_off, keys), :]
            for h, qh_ref in enumerate((q0_ref, q1_ref)):
                stage_ref[j, h] = _dot_nt(qh_ref[pl.ds(q_off, w), :], kblk)

    def value_rows(step, stage_ref):
        for j in range(NA_ROWS_PER_STEP):
            q_off, k_off, bias_row = offsets(step, j)
            vblk = v_ref[0, pl.ds(k_off, keys), :]
            valid = mask_ref[...] != 0
            outs = []
            for h in range(2):
                s = jnp.where(valid, stage_ref[j, h] + tab_ref[0, h, bias_row], MASK_VALUE)
                e = jnp.exp(s - jnp.max(s, axis=-1, keepdims=True))
                outs.append(_dot(e.astype(BF16), vblk) / jnp.sum(e, axis=-1, keepdims=True))
            o_ref[0, pl.ds(q_off, w), :] = jnp.where(head0, outs[0], outs[1]).astype(o_ref.dtype)

    score_rows(0, sa_ref)

    def step_pair(i, carry):
        even = 2 * i
        score_rows(even + 1, sb_ref)
        value_rows(even, sa_ref)
        score_rows(jnp.minimum(even + 2, n_steps - 1), sa_ref)
        value_rows(even + 1, sb_ref)
        return carry

    lax.fori_loop(0, n_steps // 2, step_pair, 0)


def _na_bias_tables(rpb):
    w = GRID_W
    cols = np.arange(w)
    col_idx = np.clip(cols[None, :] - cols[:, None], 1 - NA_WIN_COLS, NA_WIN_COLS - 1) + (NA_WIN_COLS - 1)
    cb = rpb[:, :, col_idx]
    idx = np.arange(NA_WIN_ROWS)[:, None] + np.arange(NA_WIN_ROWS)[None, :]
    t = cb[:, idx]
    t = t.transpose(0, 1, 3, 2, 4).reshape(NA_HEADS // 2, 2, NA_WIN_ROWS, w, NA_WIN_ROWS * w)
    return t


def _na_col_mask():
    w = GRID_W
    cols = np.arange(w)
    col_start = np.clip(cols - NA_WIN_COLS // 2, 0, w - NA_WIN_COLS)
    m = (cols[None, :] >= col_start[:, None]) & (cols[None, :] < col_start[:, None] + NA_WIN_COLS)
    return jnp.asarray(np.tile(m, (1, NA_WIN_ROWS)).astype(np.int32))


def _neighbourhood_attention(qkv3, q_gain, k_gain, rpb):
    bsz, seq, d3 = qkv3.shape
    d = d3 // 3
    rows = seq // GRID_W
    assert rows >= NA_WIN_ROWS and rows % (2 * NA_ROWS_PER_STEP) == 0 and d == NA_HEADS * NA_HEAD_DIM
    pair = 2 * NA_HEAD_DIM
    n_pairs = d // pair
    keys = NA_WIN_ROWS * GRID_W
    blk = lambda off: pl.BlockSpec((1, seq, pair), lambda hp, b: (b, 0, off + hp))
    gains = lambda gvec: jnp.tile(gvec, 2).reshape(1, pair)
    return pl.pallas_call(
        functools.partial(_na_body, rows=rows),
        grid=(n_pairs, bsz),
        in_specs=[blk(0), blk(n_pairs), blk(2 * n_pairs), _resident((1, pair)), _resident((1, pair)),
                  pl.BlockSpec((1, 2, NA_WIN_ROWS, GRID_W, keys), lambda hp, b: (hp, 0, 0, 0, 0)),
                  _resident((GRID_W, keys))],
        out_specs=pl.BlockSpec((1, seq, pair), lambda hp, b: (b, 0, hp)),
        out_shape=jax.ShapeDtypeStruct((bsz, seq, d), BF16),
        scratch_shapes=[pltpu.VMEM((seq, pair), BF16)] * 3
        + [pltpu.VMEM((NA_ROWS_PER_STEP, 2, GRID_W, keys), F32)] * 2,
        compiler_params=_params(2),
        name="neighbourhood_attention",
    )(qkv3, qkv3, qkv3, gains(q_gain), gains(k_gain), _na_bias_tables(rpb), _na_col_mask())


def kernel(x, mix_norm, ffn_norm, ffn_w_gate, ffn_w_up, ffn_w_down, ab_w_in, conv_w, conv_b, conv_ln_g, conv_ln_b, ssm_lambda_re, ssm_lambda_im, ssm_log_step, ssm_b_re, ssm_b_im, ssm_c_re, ssm_c_im, ssm_d, ssm_glu_w, ssm_glu_b, ab_w_out, na_w_qkv, na_q_norm, na_k_norm, na_rpb, na_w_out):
    bsz, seq, d = x.shape
    m = bsz * seq
    assert m % ROW_TILE == 0
    depth = mix_norm.shape[0]
    bf = lambda w: w.astype(BF16)
    xs = x.reshape(m, d)
    for layer in range(depth):
        i = layer // 2
        ffn = (ffn_norm[layer], bf(ffn_w_gate[layer]), bf(ffn_w_up[layer]), bf(ffn_w_down[layer]))
        if layer % 2 == 0:
            z = _norm_matmul(xs, mix_norm[layer], bf(ab_w_in[i]))
            z3 = z.reshape(bsz, seq, -1)
            ya = _conv_module(z3, conv_w[i], conv_b[i], conv_ln_g[i], conv_ln_b[i])
            ys = _s5_scan(z3[..., 2 * CONV_DIM:], ssm_lambda_re[i], ssm_lambda_im[i],
                          ssm_log_step[i], ssm_b_re[i], ssm_b_im[i], ssm_c_re[i], ssm_c_im[i])
            xs = _ssm_ffn(xs, ya.reshape(m, -1), ys.reshape(m, -1), z, ssm_d[i], bf(ssm_glu_w[i]),
                          ssm_glu_b[i], bf(ab_w_out[i]), *ffn)
        else:
            qkv = _norm_matmul(xs, mix_norm[layer], bf(na_w_qkv[i]))
            attn = _neighbourhood_attention(qkv.reshape(bsz, seq, -1), na_q_norm[i], na_k_norm[i],
                                            na_rpb[i])
            xs = _attn_ffn(xs, attn.reshape(m, d), bf(na_w_out[i]), *ffn)
    return xs.reshape(bsz, seq, d)
```

```python
import functools

import numpy as np
import jax
import jax.numpy as jnp
from jax import lax
from jax.experimental import pallas as pl
from jax.experimental.pallas import tpu as pltpu

F32 = jnp.float32
BF16 = jnp.bfloat16
HIGHEST = lax.Precision.HIGHEST

GRID_W = 64
CONV_DIM = 512
CONV_WIDTH = 31
SSM_DIM = 512
SSM_GROUP = 16
SSM_GROUPS = SSM_DIM // SSM_GROUP
SSM_STATE = 64
NA_HEADS = 16
NA_HEAD_DIM = 64
NA_WIN_ROWS = 8
NA_WIN_COLS = 16
NORM_EPS = 1e-6
LN_EPS = 1e-5
MASK_VALUE = -1e30

S5_CHUNK = 64
V7X_LANES = 128
V7X_SUBLANES = 8
V7X_VMEM_LIMIT = 56 * 1024 * 1024
ROW_TILE = 512
CONV_TILE = 64
CONV_PAD = 16
NA_ROWS_PER_STEP = 4


def _params(n_axes):
    return pltpu.CompilerParams(dimension_semantics=("arbitrary",) * n_axes,
                                vmem_limit_bytes=V7X_VMEM_LIMIT)


def _resident(shape):
    return pl.BlockSpec(shape, lambda *_: (0,) * len(shape), pipeline_mode=pl.Buffered(1))


def _rms_norm(x, gain):
    ms = jnp.mean(x * x, axis=-1, keepdims=True)
    return x * lax.rsqrt(ms + NORM_EPS) * gain


def _dot(a, b):
    return jnp.dot(a, b, preferred_element_type=F32)


def _dot_nt(a, b, precision=None):
    return lax.dot_general(a, b, (((1,), (1,)), ((), ())), precision=precision,
                           preferred_element_type=F32)


def _norm_matmul_body(x_ref, g_ref, w_ref, o_ref):
    h = _rms_norm(x_ref[...], g_ref[...]).astype(BF16)
    o_ref[...] = _dot(h, w_ref[...]).astype(o_ref.dtype)


def _norm_matmul(x, gain, w):
    m, d = x.shape
    n = w.shape[1]
    return pl.pallas_call(
        _norm_matmul_body,
        grid=(m // ROW_TILE,),
        in_specs=[pl.BlockSpec((ROW_TILE, d), lambda i: (i, 0)),
                  _resident((1, d)), _resident((d, n))],
        out_specs=pl.BlockSpec((ROW_TILE, n), lambda i: (i, 0)),
        out_shape=jax.ShapeDtypeStruct((m, n), BF16),
        compiler_params=_params(1),
        name="norm_matmul",
    )(x, gain.reshape(1, d), w)


def _ffn_tail(x1, fg_ref, wg_ref, wu_ref, wd_ref):
    h = _rms_norm(x1, fg_ref[...]).astype(BF16)
    g = _dot(h, wg_ref[...])
    u = _dot(h, wu_ref[...])
    a = (g * jax.nn.sigmoid(g) * u).astype(BF16)
    return x1 + _dot(a, wd_ref[...])


def _attn_ffn_body(x_ref, a_ref, wo_ref, fg_ref, wg_ref, wu_ref, wd_ref, o_ref):
    x1 = x_ref[...] + _dot(a_ref[...], wo_ref[...])
    o_ref[...] = _ffn_tail(x1, fg_ref, wg_ref, wu_ref, wd_ref)


def _ssm_ffn_body(x_ref, ya_ref, yt_ref, u_ref, d_ref, gw_ref, gb_ref, wo_ref,
                  fg_ref, wg_ref, wu_ref, wd_ref, o_ref):
    ys = yt_ref[...].astype(F32).reshape(SSM_DIM, -1).T
    y = ys + d_ref[...] * u_ref[...].astype(F32)
    y = jax.nn.gelu(y, approximate=True)
    gate = jax.nn.sigmoid(_dot(y.astype(BF16), gw_ref[...]) + gb_ref[...])
    yb = (y * gate).astype(BF16)
    x1 = (x_ref[...] + _dot(ya_ref[...], wo_ref[:CONV_DIM, :]) + _dot(yb, wo_ref[CONV_DIM:, :]))
    o_ref[...] = _ffn_tail(x1, fg_ref, wg_ref, wu_ref, wd_ref)


def _row_spec(width, col_block=0):
    return pl.BlockSpec((ROW_TILE, width), lambda i: (i, col_block))


def _attn_ffn(x, attn, w_out, ffn_gain, w_gate, w_up, w_down):
    m, d = x.shape
    f = w_gate.shape[1]
    return pl.pallas_call(
        _attn_ffn_body,
        grid=(m // ROW_TILE,),
        in_specs=[_row_spec(d), _row_spec(d), _resident((d, d)), _resident((1, d)),
                  _resident((d, f)), _resident((d, f)), _resident((f, d))],
        out_specs=_row_spec(d),
        out_shape=jax.ShapeDtypeStruct((m, d), F32),
        compiler_params=_params(1),
        name="attn_out_ffn",
    )(x, attn, w_out, ffn_gain.reshape(1, d), w_gate, w_up, w_down)


def _step_spec(n_chunk_rows, width, cols_per_step=1, col=0):
    return pl.BlockSpec((n_chunk_rows, width), lambda t: (0, cols_per_step * t + col))


def _slab_spec(n_chunk_rows):
    return pl.BlockSpec((SSM_GROUPS, None, SSM_GROUP, n_chunk_rows), lambda t: (0, t, 0, 0))


def _ssm_in_proj_body(x_ref, g_ref, w_ref, z_ref, ut_ref):
    h = _rms_norm(x_ref[...], g_ref[...]).astype(BF16)
    z = _dot(h, w_ref[...])
    z_ref[...] = z.astype(z_ref.dtype)
    ut = z[:, 2 * CONV_DIM:].T
    ut_ref[...] = ut.reshape(SSM_GROUPS, SSM_GROUP, -1).astype(ut_ref.dtype)


def _ssm_in_proj(xc, gain, w, chunk):
    rows = xc.shape[0]
    d = xc.shape[1] // chunk
    n = w.shape[1]
    return pl.pallas_call(
        _ssm_in_proj_body,
        grid=(chunk,),
        in_specs=[_step_spec(rows, d), _resident((1, d)), _resident((d, n))],
        out_specs=[_step_spec(rows, n), _slab_spec(rows)],
        out_shape=[jax.ShapeDtypeStruct((rows, chunk * n), BF16),
                   jax.ShapeDtypeStruct((SSM_GROUPS, chunk, SSM_GROUP, rows), BF16)],
        compiler_params=_params(1),
        name="ssm_in_proj",
    )(xc, gain.reshape(1, d), w)


def _ssm_ffn(xc, ya, yt, z, d_skip, glu_w, glu_b, w_out, ffn_gain, w_gate, w_up, w_down, chunk):
    rows = xc.shape[0]
    d = xc.shape[1] // chunk
    f = w_gate.shape[1]
    c = SSM_DIM
    return pl.pallas_call(
        _ssm_ffn_body,
        grid=(chunk,),
        in_specs=[_step_spec(rows, d), _step_spec(rows, CONV_DIM), _slab_spec(rows),
                  _step_spec(rows, c, cols_per_step=3, col=2),
                  _resident((1, c)), _resident((c, c)), _resident((1, c)),
                  _resident((CONV_DIM + c, d)), _resident((1, d)),
                  _resident((d, f)), _resident((d, f)), _resident((f, d))],
        out_specs=_step_spec(rows, d),
        out_shape=jax.ShapeDtypeStruct(xc.shape, F32),
        compiler_params=_params(1),
        name="ssm_out_ffn",
    )(xc, ya, yt, z, d_skip.reshape(1, c), glu_w, glu_b.reshape(1, c), w_out,
      ffn_gain.reshape(1, d), w_gate, w_up, w_down)


def _conv_body(za_ref, zg_ref, w_ref, cb_ref, lg_ref, lb_ref, o_ref, apad_ref, *, seq):
    c = CONV_DIM
    apad_ref[0:CONV_PAD, :] = jnp.zeros((CONV_PAD, c), F32)
    apad_ref[CONV_PAD + seq:, :] = jnp.zeros((CONV_PAD, c), F32)

    glu_tile = 256

    def glu_step(i, carry):
        r0 = pl.multiple_of(i * glu_tile, glu_tile)
        za = za_ref[0, pl.ds(r0, glu_tile), :].astype(F32)
        zg = zg_ref[0, pl.ds(r0, glu_tile), :].astype(F32)
        apad_ref[pl.ds(CONV_PAD + r0, glu_tile), :] = za * jax.nn.sigmoid(zg)
        return carry

    lax.fori_loop(0, seq // glu_tile, glu_step, 0)

    first = CONV_PAD - CONV_WIDTH // 2
    sub = V7X_SUBLANES

    def conv_step(i, carry):
        r0 = pl.multiple_of(i * CONV_TILE, CONV_TILE)
        win = apad_ref[pl.ds(r0, CONV_TILE + 2 * CONV_PAD), :]
        acc = cb_ref[...]
        for r in range(sub):
            part = None
            for k in range(r, CONV_WIDTH, sub):
                term = w_ref[k:k + 1, :] * win[k - r:k - r + CONV_TILE + sub]
                part = term if part is None else part + term
            acc = acc + part[first + r:first + r + CONV_TILE]
        mu = jnp.mean(acc, axis=-1, keepdims=True)
        xc = acc - mu
        var = jnp.mean(xc * xc, axis=-1, keepdims=True)
        y = xc * lax.rsqrt(var + LN_EPS) * lg_ref[...] + lb_ref[...]
        o_ref[0, pl.ds(r0, CONV_TILE), :] = (y * jax.nn.sigmoid(y)).astype(o_ref.dtype)
        return carry

    lax.fori_loop(0, seq // CONV_TILE, conv_step, 0)


def _conv_module(z3, conv_w, conv_b, ln_g, ln_b):
    bsz, seq, _ = z3.shape
    c = CONV_DIM
    return pl.pallas_call(
        functools.partial(_conv_body, seq=seq),
        grid=(bsz,),
        in_specs=[pl.BlockSpec((1, seq, c), lambda b: (b, 0, 0)),
                  pl.BlockSpec((1, seq, c), lambda b: (b, 0, 1)),
                  _resident((CONV_WIDTH, c)), _resident((1, c)), _resident((1, c)),
                  _resident((1, c))],
        out_specs=pl.BlockSpec((1, seq, c), lambda b: (b, 0, 0)),
        out_shape=jax.ShapeDtypeStruct((bsz, seq, c), BF16),
        scratch_shapes=[pltpu.VMEM((seq + 2 * CONV_PAD, c), F32)],
        compiler_params=_params(1),
        name="conformer_conv",
    )(z3, z3, conv_w, conv_b.reshape(1, c), ln_g.reshape(1, c), ln_b.reshape(1, c))


def _s5_body(xt_ref, par_ref, btr_ref, bti_ref, cr_ref, ci_ref, yt_ref, t_ref, w8_ref,
             *, chunk, n_chunks):
    n = SSM_STATE
    n2 = 2 * n
    p = SSM_GROUP
    r = p * chunk
    fwd = lax.broadcasted_iota(jnp.int32, (1, n2), 1) < n
    step = lax.broadcasted_iota(jnp.int32, (chunk, 1), 0).astype(F32)
    rep = (lax.broadcasted_iota(jnp.int32, (r, chunk), 0) // p
           == lax.broadcasted_iota(jnp.int32, (r, chunk), 1)).astype(F32)

    def expand(a):
        return jnp.dot(rep, a, precision=HIGHEST, preferred_element_type=F32)

    def tile_rows(a):
        return jnp.concatenate([a] * chunk, axis=0)

    lr = par_ref[0, 0:1, :]
    li = par_ref[0, 1:2, :]
    dt = jnp.exp(par_ref[0, 2:3, :])
    lr_dt = lr * dt
    li_dt = li * dt

    def cpow(e):
        mag = jnp.exp(lr_dt * e)
        ang = li_dt * e
        return mag * jnp.cos(ang), mag * jnp.sin(ang)

    ar, ai = cpow(1.0)
    den = lr * lr + li * li
    pr = ar - 1.0
    coef_re = (pr * lr + ai * li) / den
    coef_im = (ai * lr - pr * li) / den
    btr = btr_ref[0]
    bti = bti_ref[0]
    bbr = tile_rows(coef_re * btr - coef_im * bti)
    bbi = tile_rows(coef_re * bti + coef_im * btr)
    ctr = tile_rows(cr_ref[0])
    cti = tile_rows(ci_ref[0])
    e_m = jnp.where(fwd, (chunk - 1.0) - step, step)
    e_q = jnp.where(fwd, step + 1.0, chunk - step)
    pmr, pmi = (expand(v) for v in cpow(e_m))
    pqr, pqi = (expand(v) for v in cpow(e_q))
    mr = bbr * pmr - bbi * pmi
    mi = bbr * pmi + bbi * pmr
    qr = ctr * pqr - cti * pqi
    qi = ctr * pqi + cti * pqr

    def strip_rows(v):
        f = jnp.where(fwd, v, 0.0)
        b = jnp.where(fwd, 0.0, v)
        return (jnp.concatenate([f, jnp.zeros((r, n2), F32)], axis=0)
                + jnp.concatenate([jnp.zeros((r - p, n2), F32), b, jnp.zeros((p, n2), F32)], axis=0))

    wc = (_dot_nt(cr_ref[0], strip_rows(mr), HIGHEST) - _dot_nt(ci_ref[0], strip_rows(mi), HIGHEST))

    width = 2 * r - V7X_LANES
    for s in range(V7X_LANES // p):
        w8_ref[s] = wc[:, p * s:p * s + width]
    for t in range(chunk):
        a, s = divmod(p * (chunk - 1 - t), V7X_LANES)
        t_ref[p * t:p * (t + 1), :] = w8_ref[s // p, :, V7X_LANES * a:V7X_LANES * a + r].astype(BF16)

    cols = xt_ref.shape[-1]
    xt = xt_ref[0].reshape(r, cols)
    y = _dot(t_ref[...], xt)

    e = _dot(jnp.concatenate([mr, mi], axis=1).T.astype(BF16), xt)
    lvl = lax.broadcasted_iota(jnp.int32, (V7X_SUBLANES, 1), 0)
    lre, lim = cpow((chunk * jnp.left_shift(1, lvl)).astype(F32))
    pcol = jnp.concatenate([lre, lim, jnp.zeros((n2 - 2 * V7X_SUBLANES, n2), F32)], axis=0).T
    jl = lax.broadcasted_iota(jnp.int32, (1, cols), 1) % n_chunks

    prev = {}
    for d in range(2):
        rows_d = slice(d * n, (d + 1) * n)
        sr, si = e[rows_d], e[n2 + d * n:n2 + (d + 1) * n]
        shifted = lambda v, s: pltpu.roll(v, s if d == 0 else cols - s, axis=1)
        k, s = 0, 1
        while s < n_chunks:
            keep = (jl >= s) if d == 0 else (jl < n_chunks - s)
            a_r = pcol[rows_d, k:k + 1]
            a_i = pcol[rows_d, V7X_SUBLANES + k:V7X_SUBLANES + k + 1]
            rr, ri = shifted(sr, s), shifted(si, s)
            sr, si = (sr + jnp.where(keep, a_r * rr - a_i * ri, 0.0),
                      si + jnp.where(keep, a_r * ri + a_i * rr, 0.0))
            k, s = k + 1, 2 * s
        keep = (jl >= 1) if d == 0 else (jl < n_chunks - 1)
        prev[d] = (jnp.where(keep, shifted(sr, 1), 0.0), jnp.where(keep, shifted(si, 1), 0.0))

    s_prev = jnp.concatenate([prev[0][0], prev[1][0], prev[0][1], prev[1][1]], axis=0)
    q_cat = jnp.concatenate([qr, -qi], axis=1)
    y = y + _dot(q_cat.astype(BF16), s_prev.astype(BF16))
    yt_ref[0] = y.reshape(chunk, p, cols).astype(yt_ref.dtype)


def _s5_scan(ut, lam_re, lam_im, log_step, b_re, b_im, c_re, c_im, n_chunks):
    g, chunk, p, cols = ut.shape
    n2 = 2 * SSM_STATE
    assert cols % V7X_LANES == 0 and n_chunks <= 2 ** V7X_SUBLANES
    r = p * chunk
    both = lambda a: a.transpose(1, 2, 0, 3).reshape(g, -1, n2)
    par = both(jnp.stack([lam_re, lam_im, jnp.broadcast_to(log_step[..., None], lam_re.shape)],
                         axis=2))
    bt_re = both(b_re.transpose(0, 1, 3, 2))
    bt_im = both(b_im.transpose(0, 1, 3, 2))
    grp = lambda *tail: pl.BlockSpec((1,) + tail, lambda i: (i,) + (0,) * len(tail))
    return pl.pallas_call(
        functools.partial(_s5_body, chunk=chunk, n_chunks=n_chunks),
        grid=(g,),
        in_specs=[grp(chunk, p, cols), grp(3, n2), grp(p, n2), grp(p, n2), grp(p, n2), grp(p, n2)],
        out_specs=grp(chunk, p, cols),
        out_shape=jax.ShapeDtypeStruct(ut.shape, BF16),
        scratch_shapes=[pltpu.VMEM((r, r), BF16),
                        pltpu.VMEM((V7X_LANES // p, p, 2 * r - V7X_LANES), F32)],
        compiler_params=_params(1),
        name="s5_chunked",
    )(ut, par, bt_re, bt_im, both(c_re), both(c_im))


def _na_body(q_ref, k_ref, v_ref, qg_ref, kg_ref, tab_ref, mask_ref, o_ref,
             q0_ref, q1_ref, kn_ref, sa_ref, sb_ref, *, rows):
    w = GRID_W
    keys = NA_WIN_ROWS * w
    pair = 2 * NA_HEAD_DIM
    head0 = lax.broadcasted_iota(jnp.int32, (1, pair), 1) < NA_HEAD_DIM
    same_head = (lax.broadcasted_iota(jnp.int32, (pair, pair), 0) // NA_HEAD_DIM
                 == lax.broadcasted_iota(jnp.int32, (pair, pair), 1) // NA_HEAD_DIM).astype(BF16)

    def head_norm(t, gain):
        sq = t * t
        hi = sq.astype(BF16)
        lo = (sq - hi.astype(F32)).astype(BF16)
        ms = (_dot(hi, same_head) + _dot(lo, same_head)) / NA_HEAD_DIM
        return t * lax.rsqrt(ms + NORM_EPS) * gain

    qn = head_norm(q_ref[0].astype(F32), qg_ref[...]) * (NA_HEAD_DIM ** -0.5)
    q0_ref[...] = jnp.where(head0, qn, 0.0).astype(BF16)
    q1_ref[...] = jnp.where(head0, 0.0, qn).astype(BF16)
    kn_ref[...] = head_norm(k_ref[0].astype(F32), kg_ref[...]).astype(BF16)

    n_steps = rows // NA_ROWS_PER_STEP

    def offsets(step, j):
        row = step * NA_ROWS_PER_STEP + j
        start = jnp.clip(row - NA_WIN_ROWS // 2, 0, rows - NA_WIN_ROWS)
        return (pl.multiple_of(row * w, w), pl.multiple_of(start * w, w),
                start - row + (NA_WIN_ROWS - 1))

    def score_rows(step, stage_ref):
        for j in range(NA_ROWS_PER_STEP):
            q_off, k_off, _ = offsets(step, j)
            kblk = kn_ref[pl.ds(k_off, keys), :]
            for h, qh_ref in enumerate((q0_ref, q1_ref)):
                stage_ref[j, h] = _dot_nt(qh_ref[pl.ds(q_off, w), :], kblk)

    def value_rows(step, stage_ref):
        for j in range(NA_ROWS_PER_STEP):
            q_off, k_off, bias_row = offsets(step, j)
            vblk = v_ref[0, pl.ds(k_off, keys), :]
            valid = mask_ref[...] != 0
            outs = []
            for h in range(2):
                s = jnp.where(valid, stage_ref[j, h] + tab_ref[0, h, bias_row], MASK_VALUE)
                e = jnp.exp(s - jnp.max(s, axis=-1, keepdims=True))
                outs.append(_dot(e.astype(BF16), vblk) / jnp.sum(e, axis=-1, keepdims=True))
            o_ref[0, pl.ds(q_off, w), :] = jnp.where(head0, outs[0], outs[1]).astype(o_ref.dtype)

    score_rows(0, sa_ref)

    def step_pair(i, carry):
        even = 2 * i
        score_rows(even + 1, sb_ref)
        value_rows(even, sa_ref)
        score_rows(jnp.minimum(even + 2, n_steps - 1), sa_ref)
        value_rows(even + 1, sb_ref)
        return carry

    lax.fori_loop(0, n_steps // 2, step_pair, 0)


def _na_bias_tables(rpb):
    w = GRID_W
    cols = np.arange(w)
    col_idx = np.clip(cols[None, :] - cols[:, None], 1 - NA_WIN_COLS, NA_WIN_COLS - 1) + (NA_WIN_COLS - 1)
    cb = rpb[:, :, col_idx]
    idx = np.arange(NA_WIN_ROWS)[:, None] + np.arange(NA_WIN_ROWS)[None, :]
    t = cb[:, idx]
    t = t.transpose(0, 1, 3, 2, 4).reshape(NA_HEADS // 2, 2, NA_WIN_ROWS, w, NA_WIN_ROWS * w)
    return t


def _na_col_mask():
    w = GRID_W
    cols = np.arange(w)
    col_start = np.clip(cols - NA_WIN_COLS // 2, 0, w - NA_WIN_COLS)
    m = (cols[None, :] >= col_start[:, None]) & (cols[None, :] < col_start[:, None] + NA_WIN_COLS)
    return jnp.asarray(np.tile(m, (1, NA_WIN_ROWS)).astype(np.int32))


def _neighbourhood_attention(qkv3, q_gain, k_gain, rpb):
    bsz, seq, d3 = qkv3.shape
    d = d3 // 3
    rows = seq // GRID_W
    assert rows >= NA_WIN_ROWS and rows % (2 * NA_ROWS_PER_STEP) == 0 and d == NA_HEADS * NA_HEAD_DIM
    pair = 2 * NA_HEAD_DIM
    n_pairs = d // pair
    keys = NA_WIN_ROWS * GRID_W
    blk = lambda off: pl.BlockSpec((1, seq, pair), lambda hp, b: (b, 0, off + hp))
    gains = lambda gvec: jnp.tile(gvec, 2).reshape(1, pair)
    return pl.pallas_call(
        functools.partial(_na_body, rows=rows),
        grid=(n_pairs, bsz),
        in_specs=[blk(0), blk(n_pairs), blk(2 * n_pairs), _resident((1, pair)), _resident((1, pair)),
                  pl.BlockSpec((1, 2, NA_WIN_ROWS, GRID_W, keys), lambda hp, b: (hp, 0, 0, 0, 0)),
                  _resident((GRID_W, keys))],
        out_specs=pl.BlockSpec((1, seq, pair), lambda hp, b: (b, 0, hp)),
        out_shape=jax.ShapeDtypeStruct((bsz, seq, d), BF16),
        scratch_shapes=[pltpu.VMEM((seq, pair), BF16)] * 3
        + [pltpu.VMEM((NA_ROWS_PER_STEP, 2, GRID_W, keys), F32)] * 2,
        compiler_params=_params(2),
        name="neighbourhood_attention",
    )(qkv3, qkv3, qkv3, gains(q_gain), gains(k_gain), _na_bias_tables(rpb), _na_col_mask())


def kernel(x, mix_norm, ffn_norm, ffn_w_gate, ffn_w_up, ffn_w_down, ab_w_in, conv_w, conv_b, conv_ln_g, conv_ln_b, ssm_lambda_re, ssm_lambda_im, ssm_log_step, ssm_b_re, ssm_b_im, ssm_c_re, ssm_c_im, ssm_d, ssm_glu_w, ssm_glu_b, ab_w_out, na_w_qkv, na_q_norm, na_k_norm, na_rpb, na_w_out):
    bsz, seq, d = x.shape
    m = bsz * seq
    chunk = S5_CHUNK
    assert m % ROW_TILE == 0 and seq % chunk == 0
    n_chunks = seq // chunk
    rows_c = bsz * n_chunks
    depth = mix_norm.shape[0]
    bf = lambda w: w.astype(BF16)
    xs = x.reshape(m, d)
    for layer in range(depth):
        i = layer // 2
        ffn = (ffn_norm[layer], bf(ffn_w_gate[layer]), bf(ffn_w_up[layer]), bf(ffn_w_down[layer]))
        if layer % 2 == 0:
            xc = xs.reshape(rows_c, chunk * d)
            z, ut = _ssm_in_proj(xc, mix_norm[layer], bf(ab_w_in[i]), chunk)
            ya = _conv_module(z.reshape(bsz, seq, -1), conv_w[i], conv_b[i], conv_ln_g[i],
                              conv_ln_b[i])
            yt = _s5_scan(ut, ssm_lambda_re[i], ssm_lambda_im[i], ssm_log_step[i], ssm_b_re[i],
                          ssm_b_im[i], ssm_c_re[i], ssm_c_im[i], n_chunks)
            xs = _ssm_ffn(xc, ya.reshape(rows_c, -1), yt, z, ssm_d[i], bf(ssm_glu_w[i]),
                          ssm_glu_b[i], bf(ab_w_out[i]), *ffn, chunk).reshape(m, d)
        else:
            qkv = _norm_matmul(xs, mix_norm[layer], bf(na_w_qkv[i]))
            attn = _neighbourhood_attention(qkv.reshape(bsz, seq, -1), na_q_norm[i], na_k_norm[i],
                                            na_rpb[i])
            xs = _attn_ffn(xs, attn.reshape(m, d), bf(na_w_out[i]), *ffn)
    return xs.reshape(bsz, seq, d)
```

```python
import functools

import numpy as np
import jax
import jax.numpy as jnp
from jax import lax
from jax.experimental import pallas as pl
from jax.experimental.pallas import tpu as pltpu

F32 = jnp.float32
BF16 = jnp.bfloat16
HIGHEST = lax.Precision.HIGHEST

GRID_W = 64
CONV_DIM = 512
CONV_WIDTH = 31
SSM_DIM = 512
SSM_GROUP = 16
SSM_GROUPS = SSM_DIM // SSM_GROUP
SSM_STATE = 64
NA_HEADS = 16
NA_HEAD_DIM = 64
NA_WIN_ROWS = 8
NA_WIN_COLS = 16
NORM_EPS = 1e-6
LN_EPS = 1e-5
MASK_VALUE = -1e30

S5_CHUNK = 64
S5_STEP_BLOCK = 8
S5_CHUNK_BLOCK = 128
V7X_LANES = 128
V7X_SUBLANES = 8
V7X_VMEM_LIMIT = 56 * 1024 * 1024
ROW_TILE = 512
CONV_TILE = 64
CONV_PAD = 16
NA_ROWS_PER_STEP = 4


def _params(n_axes):
    return pltpu.CompilerParams(dimension_semantics=("arbitrary",) * n_axes,
                                vmem_limit_bytes=V7X_VMEM_LIMIT)


def _resident(shape):
    return pl.BlockSpec(shape, lambda *_: (0,) * len(shape), pipeline_mode=pl.Buffered(1))


def _rms_norm(x, gain):
    ms = jnp.mean(x * x, axis=-1, keepdims=True)
    return x * lax.rsqrt(ms + NORM_EPS) * gain


def _dot(a, b):
    return jnp.dot(a, b, preferred_element_type=F32)


def _dot_nt(a, b, precision=None):
    return lax.dot_general(a, b, (((1,), (1,)), ((), ())), precision=precision,
                           preferred_element_type=F32)


def _norm_matmul_body(x_ref, g_ref, w_ref, o_ref):
    h = _rms_norm(x_ref[...], g_ref[...]).astype(BF16)
    o_ref[...] = _dot(h, w_ref[...]).astype(o_ref.dtype)


def _norm_matmul(x, gain, w):
    m, d = x.shape
    n = w.shape[1]
    return pl.pallas_call(
        _norm_matmul_body,
        grid=(m // ROW_TILE,),
        in_specs=[pl.BlockSpec((ROW_TILE, d), lambda i: (i, 0)),
                  _resident((1, d)), _resident((d, n))],
        out_specs=pl.BlockSpec((ROW_TILE, n), lambda i: (i, 0)),
        out_shape=jax.ShapeDtypeStruct((m, n), BF16),
        compiler_params=_params(1),
        name="norm_matmul",
    )(x, gain.reshape(1, d), w)


def _ffn_tail(x1, fg_ref, wg_ref, wu_ref, wd_ref):
    h = _rms_norm(x1, fg_ref[...]).astype(BF16)
    g = _dot(h, wg_ref[...])
    u = _dot(h, wu_ref[...])
    a = (g * jax.nn.sigmoid(g) * u).astype(BF16)
    return x1 + _dot(a, wd_ref[...])


def _attn_ffn_body(x_ref, a_ref, wo_ref, fg_ref, wg_ref, wu_ref, wd_ref, o_ref):
    x1 = x_ref[...] + _dot(a_ref[...], wo_ref[...])
    o_ref[...] = _ffn_tail(x1, fg_ref, wg_ref, wu_ref, wd_ref)


def _ssm_ffn_body(x_ref, ya_ref, ys_ref, u_ref, d_ref, gw_ref, gb_ref, wo_ref,
                  fg_ref, wg_ref, wu_ref, wd_ref, o_ref):
    y = ys_ref[...] + d_ref[...] * u_ref[...]
    y = jax.nn.gelu(y, approximate=True)
    gate = jax.nn.sigmoid(_dot(y.astype(BF16), gw_ref[...]) + gb_ref[...])
    yb = (y * gate).astype(BF16)
    x1 = (x_ref[...] + _dot(ya_ref[...], wo_ref[:CONV_DIM, :]) + _dot(yb, wo_ref[CONV_DIM:, :]))
    o_ref[...] = _ffn_tail(x1, fg_ref, wg_ref, wu_ref, wd_ref)


def _row_spec(width, col_block=0):
    return pl.BlockSpec((ROW_TILE, width), lambda i: (i, col_block))


def _attn_ffn(x, attn, w_out, ffn_gain, w_gate, w_up, w_down):
    m, d = x.shape
    f = w_gate.shape[1]
    return pl.pallas_call(
        _attn_ffn_body,
        grid=(m // ROW_TILE,),
        in_specs=[_row_spec(d), _row_spec(d), _resident((d, d)), _resident((1, d)),
                  _resident((d, f)), _resident((d, f)), _resident((f, d))],
        out_specs=_row_spec(d),
        out_shape=jax.ShapeDtypeStruct((m, d), F32),
        compiler_params=_params(1),
        name="attn_out_ffn",
    )(x, attn, w_out, ffn_gain.reshape(1, d), w_gate, w_up, w_down)


def _ssm_ffn(x, ya, ys, z, d_skip, glu_w, glu_b, w_out, ffn_gain, w_gate, w_up, w_down):
    m, d = x.shape
    f = w_gate.shape[1]
    c = SSM_DIM
    return pl.pallas_call(
        _ssm_ffn_body,
        grid=(m // ROW_TILE,),
        in_specs=[_row_spec(d), _row_spec(CONV_DIM), _row_spec(c),
                  _row_spec(c, col_block=2 * CONV_DIM // c),
                  _resident((1, c)), _resident((c, c)), _resident((1, c)),
                  _resident((CONV_DIM + c, d)), _resident((1, d)),
                  _resident((d, f)), _resident((d, f)), _resident((f, d))],
        out_specs=_row_spec(d),
        out_shape=jax.ShapeDtypeStruct((m, d), F32),
        compiler_params=_params(1),
        name="ssm_out_ffn",
    )(x, ya, ys, z, d_skip.reshape(1, c), glu_w, glu_b.reshape(1, c), w_out,
      ffn_gain.reshape(1, d), w_gate, w_up, w_down)


def _token_block_spec(width):
    return pl.BlockSpec((S5_CHUNK_BLOCK, S5_STEP_BLOCK, width), lambda a, cb: (cb, a, 0))


def _slab_block_spec():
    return pl.BlockSpec((SSM_GROUPS, S5_STEP_BLOCK, SSM_GROUP, S5_CHUNK_BLOCK),
                        lambda a, cb: (0, a, 0, cb))


def _ssm_in_proj_body(x_ref, g_ref, w_ref, z_ref, ut_ref, u_scr):
    cb, st, d = x_ref.shape
    h = _rms_norm(x_ref[...].reshape(cb * st, d), g_ref[...]).astype(BF16)
    z = _dot(h, w_ref[...])
    z_ref[...] = z.reshape(cb, st, -1)
    lanes = V7X_LANES
    groups = lanes // SSM_GROUP
    for l in range(SSM_DIM // lanes):
        u_scr[l] = z[:, 2 * CONV_DIM + l * lanes:2 * CONV_DIM + (l + 1) * lanes]
        for t in range(st):
            ut = u_scr[l, pl.ds(t, cb, stride=st), :].T
            ut_ref[l * groups:(l + 1) * groups, t] = (
                ut.reshape(groups, SSM_GROUP, cb).astype(ut_ref.dtype))


def _ssm_in_proj(xc, gain, w):
    rows, chunk, d = xc.shape
    n = w.shape[1]
    assert rows % S5_CHUNK_BLOCK == 0 and chunk % S5_STEP_BLOCK == 0
    return pl.pallas_call(
        _ssm_in_proj_body,
        grid=(chunk // S5_STEP_BLOCK, rows // S5_CHUNK_BLOCK),
        in_specs=[_token_block_spec(d), _resident((1, d)), _resident((d, n))],
        out_specs=[_token_block_spec(n), _slab_block_spec()],
        out_shape=[jax.ShapeDtypeStruct((rows, chunk, n), F32),
                   jax.ShapeDtypeStruct((SSM_GROUPS, chunk, SSM_GROUP, rows), BF16)],
        scratch_shapes=[pltpu.VMEM((SSM_DIM // V7X_LANES, S5_CHUNK_BLOCK * S5_STEP_BLOCK, V7X_LANES),
                                   F32)],
        compiler_params=_params(2),
        name="ssm_in_proj",
    )(xc, gain.reshape(1, d), w)


def _s5_unlayout_body(yt_ref, ys_ref, y_scr):
    g, st, p, cb = yt_ref.shape
    lanes = V7X_LANES
    groups = lanes // p
    for l in range(g * p // lanes):
        for t in range(st):
            y = yt_ref[l * groups:(l + 1) * groups, t].astype(F32).reshape(lanes, cb).T
            y_scr[l, pl.ds(t, cb, stride=st), :] = y
        ys_ref[:, :, l * lanes:(l + 1) * lanes] = y_scr[l].reshape(cb, st, lanes)


def _s5_unlayout(yt):
    g, chunk, p, rows = yt.shape
    return pl.pallas_call(
        _s5_unlayout_body,
        grid=(chunk // S5_STEP_BLOCK, rows // S5_CHUNK_BLOCK),
        in_specs=[_slab_block_spec()],
        out_specs=_token_block_spec(g * p),
        out_shape=jax.ShapeDtypeStruct((rows, chunk, g * p), F32),
        scratch_shapes=[pltpu.VMEM((g * p // V7X_LANES, S5_CHUNK_BLOCK * S5_STEP_BLOCK, V7X_LANES),
                                   F32)],
        compiler_params=_params(2),
        name="s5_unlayout",
    )(yt)


def _conv_body(za_ref, zg_ref, w_ref, cb_ref, lg_ref, lb_ref, o_ref, apad_ref, *, seq):
    c = CONV_DIM
    apad_ref[0:CONV_PAD, :] = jnp.zeros((CONV_PAD, c), F32)
    apad_ref[CONV_PAD + seq:, :] = jnp.zeros((CONV_PAD, c), F32)

    glu_tile = 256

    def glu_step(i, carry):
        r0 = pl.multiple_of(i * glu_tile, glu_tile)
        za = za_ref[0, pl.ds(r0, glu_tile), :].astype(F32)
        zg = zg_ref[0, pl.ds(r0, glu_tile), :].astype(F32)
        apad_ref[pl.ds(CONV_PAD + r0, glu_tile), :] = za * jax.nn.sigmoid(zg)
        return carry

    lax.fori_loop(0, seq // glu_tile, glu_step, 0)

    first = CONV_PAD - CONV_WIDTH // 2
    sub = V7X_SUBLANES

    def conv_step(i, carry):
        r0 = pl.multiple_of(i * CONV_TILE, CONV_TILE)
        win = apad_ref[pl.ds(r0, CONV_TILE + 2 * CONV_PAD), :]
        acc = cb_ref[...]
        for r in range(sub):
            part = None
            for k in range(r, CONV_WIDTH, sub):
                term = w_ref[k:k + 1, :] * win[k - r:k - r + CONV_TILE + sub]
                part = term if part is None else part + term
            acc = acc + part[first + r:first + r + CONV_TILE]
        mu = jnp.mean(acc, axis=-1, keepdims=True)
        xc = acc - mu
        var = jnp.mean(xc * xc, axis=-1, keepdims=True)
        y = xc * lax.rsqrt(var + LN_EPS) * lg_ref[...] + lb_ref[...]
        o_ref[0, pl.ds(r0, CONV_TILE), :] = (y * jax.nn.sigmoid(y)).astype(o_ref.dtype)
        return carry

    lax.fori_loop(0, seq // CONV_TILE, conv_step, 0)


def _conv_module(z3, conv_w, conv_b, ln_g, ln_b):
    bsz, seq, _ = z3.shape
    c = CONV_DIM
    return pl.pallas_call(
        functools.partial(_conv_body, seq=seq),
        grid=(bsz,),
        in_specs=[pl.BlockSpec((1, seq, c), lambda b: (b, 0, 0)),
                  pl.BlockSpec((1, seq, c), lambda b: (b, 0, 1)),
                  _resident((CONV_WIDTH, c)), _resident((1, c)), _resident((1, c)),
                  _resident((1, c))],
        out_specs=pl.BlockSpec((1, seq, c), lambda b: (b, 0, 0)),
        out_shape=jax.ShapeDtypeStruct((bsz, seq, c), BF16),
        scratch_shapes=[pltpu.VMEM((seq + 2 * CONV_PAD, c), F32)],
        compiler_params=_params(1),
        name="conformer_conv",
    )(z3, z3, conv_w, conv_b.reshape(1, c), ln_g.reshape(1, c), ln_b.reshape(1, c))


def _s5_body(xt_ref, par_ref, btr_ref, bti_ref, cr_ref, ci_ref, yt_ref, t_ref, w8_ref,
             *, chunk, n_chunks):
    n = SSM_STATE
    n2 = 2 * n
    p = SSM_GROUP
    r = p * chunk
    fwd = lax.broadcasted_iota(jnp.int32, (1, n2), 1) < n
    step = lax.broadcasted_iota(jnp.int32, (chunk, 1), 0).astype(F32)
    rep = (lax.broadcasted_iota(jnp.int32, (r, chunk), 0) // p
           == lax.broadcasted_iota(jnp.int32, (r, chunk), 1)).astype(F32)

    def expand(a):
        return jnp.dot(rep, a, precision=HIGHEST, preferred_element_type=F32)

    def tile_rows(a):
        return jnp.concatenate([a] * chunk, axis=0)

    lr = par_ref[0, 0:1, :]
    li = par_ref[0, 1:2, :]
    dt = jnp.exp(par_ref[0, 2:3, :])
    lr_dt = lr * dt
    li_dt = li * dt

    def cpow(e):
        mag = jnp.exp(lr_dt * e)
        ang = li_dt * e
        return mag * jnp.cos(ang), mag * jnp.sin(ang)

    ar, ai = cpow(1.0)
    den = lr * lr + li * li
    pr = ar - 1.0
    coef_re = (pr * lr + ai * li) / den
    coef_im = (ai * lr - pr * li) / den
    btr = btr_ref[0]
    bti = bti_ref[0]
    bbr = tile_rows(coef_re * btr - coef_im * bti)
    bbi = tile_rows(coef_re * bti + coef_im * btr)
    ctr = tile_rows(cr_ref[0])
    cti = tile_rows(ci_ref[0])
    e_m = jnp.where(fwd, (chunk - 1.0) - step, step)
    e_q = jnp.where(fwd, step + 1.0, chunk - step)
    pmr, pmi = (expand(v) for v in cpow(e_m))
    pqr, pqi = (expand(v) for v in cpow(e_q))
    mr = bbr * pmr - bbi * pmi
    mi = bbr * pmi + bbi * pmr
    qr = ctr * pqr - cti * pqi
    qi = ctr * pqi + cti * pqr

    def strip_rows(v):
        f = jnp.where(fwd, v, 0.0)
        b = jnp.where(fwd, 0.0, v)
        return (jnp.concatenate([f, jnp.zeros((r, n2), F32)], axis=0)
                + jnp.concatenate([jnp.zeros((r - p, n2), F32), b, jnp.zeros((p, n2), F32)], axis=0))

    wc = (_dot_nt(cr_ref[0], strip_rows(mr), HIGHEST) - _dot_nt(ci_ref[0], strip_rows(mi), HIGHEST))

    width = 2 * r - V7X_LANES
    for s in range(V7X_LANES // p):
        w8_ref[s] = wc[:, p * s:p * s + width]
    for t in range(chunk):
        a, s = divmod(p * (chunk - 1 - t), V7X_LANES)
        t_ref[p * t:p * (t + 1), :] = w8_ref[s // p, :, V7X_LANES * a:V7X_LANES * a + r].astype(BF16)

    cols = xt_ref.shape[-1]
    xt = xt_ref[0].reshape(r, cols)
    y = _dot(t_ref[...], xt)

    e = _dot(jnp.concatenate([mr, mi], axis=1).T.astype(BF16), xt)
    lvl = lax.broadcasted_iota(jnp.int32, (V7X_SUBLANES, 1), 0)
    lre, lim = cpow((chunk * jnp.left_shift(1, lvl)).astype(F32))
    pcol = jnp.concatenate([lre, lim, jnp.zeros((n2 - 2 * V7X_SUBLANES, n2), F32)], axis=0).T
    jl = lax.broadcasted_iota(jnp.int32, (1, cols), 1) % n_chunks

    prev = {}
    for d in range(2):
        rows_d = slice(d * n, (d + 1) * n)
        sr, si = e[rows_d], e[n2 + d * n:n2 + (d + 1) * n]
        shifted = lambda v, s: pltpu.roll(v, s if d == 0 else cols - s, axis=1)
        k, s = 0, 1
        while s < n_chunks:
            keep = (jl >= s) if d == 0 else (jl < n_chunks - s)
            a_r = pcol[rows_d, k:k + 1]
            a_i = pcol[rows_d, V7X_SUBLANES + k:V7X_SUBLANES + k + 1]
            rr, ri = shifted(sr, s), shifted(si, s)
            sr, si = (sr + jnp.where(keep, a_r * rr - a_i * ri, 0.0),
                      si + jnp.where(keep, a_r * ri + a_i * rr, 0.0))
            k, s = k + 1, 2 * s
        keep = (jl >= 1) if d == 0 else (jl < n_chunks - 1)
        prev[d] = (jnp.where(keep, shifted(sr, 1), 0.0), jnp.where(keep, shifted(si, 1), 0.0))

    s_prev = jnp.concatenate([prev[0][0], prev[1][0], prev[0][1], prev[1][1]], axis=0)
    q_cat = jnp.concatenate([qr, -qi], axis=1)
    y = y + _dot(q_cat.astype(BF16), s_prev.astype(BF16))
    yt_ref[0] = y.reshape(chunk, p, cols).astype(yt_ref.dtype)


def _s5_scan(ut, lam_re, lam_im, log_step, b_re, b_im, c_re, c_im, n_chunks):
    g, chunk, p, cols = ut.shape
    n2 = 2 * SSM_STATE
    assert cols % V7X_LANES == 0 and n_chunks <= 2 ** V7X_SUBLANES
    r = p * chunk
    both = lambda a: a.transpose(1, 2, 0, 3).reshape(g, -1, n2)
    par = both(jnp.stack([lam_re, lam_im, jnp.broadcast_to(log_step[..., None], lam_re.shape)],
                         axis=2))
    bt_re = both(b_re.transpose(0, 1, 3, 2))
    bt_im = both(b_im.transpose(0, 1, 3, 2))
    grp = lambda *tail: pl.BlockSpec((1,) + tail, lambda i: (i,) + (0,) * len(tail))
    return pl.pallas_call(
        functools.partial(_s5_body, chunk=chunk, n_chunks=n_chunks),
        grid=(g,),
        in_specs=[grp(chunk, p, cols), grp(3, n2), grp(p, n2), grp(p, n2), grp(p, n2), grp(p, n2)],
        out_specs=grp(chunk, p, cols),
        out_shape=jax.ShapeDtypeStruct(ut.shape, BF16),
        scratch_shapes=[pltpu.VMEM((r, r), BF16),
                        pltpu.VMEM((V7X_LANES // p, p, 2 * r - V7X_LANES), F32)],
        compiler_params=_params(1),
        name="s5_chunked",
    )(ut, par, bt_re, bt_im, both(c_re), both(c_im))


def _na_body(q_ref, k_ref, v_ref, qg_ref, kg_ref, tab_ref, mask_ref, o_ref,
             q0_ref, q1_ref, kn_ref, sa_ref, sb_ref, *, rows):
    w = GRID_W
    keys = NA_WIN_ROWS * w
    pair = 2 * NA_HEAD_DIM
    head0 = lax.broadcasted_iota(jnp.int32, (1, pair), 1) < NA_HEAD_DIM
    same_head = (lax.broadcasted_iota(jnp.int32, (pair, pair), 0) // NA_HEAD_DIM
                 == lax.broadcasted_iota(jnp.int32, (pair, pair), 1) // NA_HEAD_DIM).astype(BF16)

    def head_norm(t, gain):
        sq = t * t
        hi = sq.astype(BF16)
        lo = (sq - hi.astype(F32)).astype(BF16)
        ms = (_dot(hi, same_head) + _dot(lo, same_head)) / NA_HEAD_DIM
        return t * lax.rsqrt(ms + NORM_EPS) * gain

    qn = head_norm(q_ref[0].astype(F32), qg_ref[...]) * (NA_HEAD_DIM ** -0.5)
    q0_ref[...] = jnp.where(head0, qn, 0.0).astype(BF16)
    q1_ref[...] = jnp.where(head0, 0.0, qn).astype(BF16)
    kn_ref[...] = head_norm(k_ref[0].astype(F32), kg_ref[...]).astype(BF16)

    n_steps = rows // NA_ROWS_PER_STEP

    def offsets(step, j):
        row = step * NA_ROWS_PER_STEP + j
        start = jnp.clip(row - NA_WIN_ROWS // 2, 0, rows - NA_WIN_ROWS)
        return (pl.multiple_of(row * w, w), pl.multiple_of(start * w, w),
                start - row + (NA_WIN_ROWS - 1))

    def score_rows(step, stage_ref):
        for j in range(NA_ROWS_PER_STEP):
            q_off, k_off, _ = offsets(step, j)
            kblk = kn_ref[pl.ds(k_off, keys), :]
            for h, qh_ref in enumerate((q0_ref, q1_ref)):
                stage_ref[j, h] = _dot_nt(qh_ref[pl.ds(q_off, w), :], kblk)

    def value_rows(step, stage_ref):
        for j in range(NA_ROWS_PER_STEP):
            q_off, k_off, bias_row = offsets(step, j)
            vblk = v_ref[0, pl.ds(k_off, keys), :]
            valid = mask_ref[...] != 0
            outs = []
            for h in range(2):
                s = jnp.where(valid, stage_ref[j, h] + tab_ref[0, h, bias_row], MASK_VALUE)
                e = jnp.exp(s - jnp.max(s, axis=-1, keepdims=True))
                outs.append(_dot(e.astype(BF16), vblk) / jnp.sum(e, axis=-1, keepdims=True))
            o_ref[0, pl.ds(q_off, w), :] = jnp.where(head0, outs[0], outs[1]).astype(o_ref.dtype)

    score_rows(0, sa_ref)

    def step_pair(i, carry):
        even = 2 * i
        score_rows(even + 1, sb_ref)
        value_rows(even, sa_ref)
        score_rows(jnp.minimum(even + 2, n_steps - 1), sa_ref)
        value_rows(even + 1, sb_ref)
        return carry

    lax.fori_loop(0, n_steps // 2, step_pair, 0)


def _na_bias_tables(rpb):
    w = GRID_W
    cols = np.arange(w)
    col_idx = np.clip(cols[None, :] - cols[:, None], 1 - NA_WIN_COLS, NA_WIN_COLS - 1) + (NA_WIN_COLS - 1)
    cb = rpb[:, :, col_idx]
    idx = np.arange(NA_WIN_ROWS)[:, None] + np.arange(NA_WIN_ROWS)[None, :]
    t = cb[:, idx]
    t = t.transpose(0, 1, 3, 2, 4).reshape(NA_HEADS // 2, 2, NA_WIN_ROWS, w, NA_WIN_ROWS * w)
    return t


def _na_col_mask():
    w = GRID_W
    cols = np.arange(w)
    col_start = np.clip(cols - NA_WIN_COLS // 2, 0, w - NA_WIN_COLS)
    m = (cols[None, :] >= col_start[:, None]) & (cols[None, :] < col_start[:, None] + NA_WIN_COLS)
    return jnp.asarray(np.tile(m, (1, NA_WIN_ROWS)).astype(np.int32))


def _neighbourhood_attention(qkv3, q_gain, k_gain, rpb):
    bsz, seq, d3 = qkv3.shape
    d = d3 // 3
    rows = seq // GRID_W
    assert rows >= NA_WIN_ROWS and rows % (2 * NA_ROWS_PER_STEP) == 0 and d == NA_HEADS * NA_HEAD_DIM
    pair = 2 * NA_HEAD_DIM
    n_pairs = d // pair
    keys = NA_WIN_ROWS * GRID_W
    blk = lambda off: pl.BlockSpec((1, seq, pair), lambda hp, b: (b, 0, off + hp))
    gains = lambda gvec: jnp.tile(gvec, 2).reshape(1, pair)
    return pl.pallas_call(
        functools.partial(_na_body, rows=rows),
        grid=(n_pairs, bsz),
        in_specs=[blk(0), blk(n_pairs), blk(2 * n_pairs), _resident((1, pair)), _resident((1, pair)),
                  pl.BlockSpec((1, 2, NA_WIN_ROWS, GRID_W, keys), lambda hp, b: (hp, 0, 0, 0, 0)),
                  _resident((GRID_W, keys))],
        out_specs=pl.BlockSpec((1, seq, pair), lambda hp, b: (b, 0, hp)),
        out_shape=jax.ShapeDtypeStruct((bsz, seq, d), BF16),
        scratch_shapes=[pltpu.VMEM((seq, pair), BF16)] * 3
        + [pltpu.VMEM((NA_ROWS_PER_STEP, 2, GRID_W, keys), F32)] * 2,
        compiler_params=_params(2),
        name="neighbourhood_attention",
    )(qkv3, qkv3, qkv3, gains(q_gain), gains(k_gain), _na_bias_tables(rpb), _na_col_mask())


def kernel(x, mix_norm, ffn_norm, ffn_w_gate, ffn_w_up, ffn_w_down, ab_w_in, conv_w, conv_b, conv_ln_g, conv_ln_b, ssm_lambda_re, ssm_lambda_im, ssm_log_step, ssm_b_re, ssm_b_im, ssm_c_re, ssm_c_im, ssm_d, ssm_glu_w, ssm_glu_b, ab_w_out, na_w_qkv, na_q_norm, na_k_norm, na_rpb, na_w_out):
    bsz, seq, d = x.shape
    m = bsz * seq
    chunk = S5_CHUNK
    assert m % ROW_TILE == 0 and seq % chunk == 0
    n_chunks = seq // chunk
    rows_c = bsz * n_chunks
    depth = mix_norm.shape[0]
    bf = lambda w: w.astype(BF16)
    xs = x.reshape(m, d)
    for layer in range(depth):
        i = layer // 2
        ffn = (ffn_norm[layer], bf(ffn_w_gate[layer]), bf(ffn_w_up[layer]), bf(ffn_w_down[layer]))
        if layer % 2 == 0:
            xc = xs.reshape(rows_c, chunk, d)
            z, ut = _ssm_in_proj(xc, mix_norm[layer], bf(ab_w_in[i]))
            ya = _conv_module(z.reshape(bsz, seq, -1), conv_w[i], conv_b[i], conv_ln_g[i],
                              conv_ln_b[i])
            yt = _s5_scan(ut, ssm_lambda_re[i], ssm_lambda_im[i], ssm_log_step[i], ssm_b_re[i],
                          ssm_b_im[i], ssm_c_re[i], ssm_c_im[i], n_chunks)
            ys = _s5_unlayout(yt)
            xs = _ssm_ffn(xs, ya.reshape(m, -1), ys.reshape(m, -1), z.reshape(m, -1), ssm_d[i],
                          bf(ssm_glu_w[i]), ssm_glu_b[i], bf(ab_w_out[i]), *ffn)
        else:
            qkv = _norm_matmul(xs, mix_norm[layer], bf(na_w_qkv[i]))
            attn = _neighbourhood_attention(qkv.reshape(bsz, seq, -1), na_q_norm[i], na_k_norm[i],
                                            na_rpb[i])
            xs = _attn_ffn(xs, attn.reshape(m, d), bf(na_w_out[i]), *ffn)
    return xs.reshape(bsz, seq, d)
```

```python
import functools

import numpy as np
import jax
import jax.numpy as jnp
from jax import lax
from jax.experimental import pallas as pl
from jax.experimental.pallas import tpu as pltpu

F32 = jnp.float32
BF16 = jnp.bfloat16

GRID_W = 64
CONV_DIM = 512
CONV_WIDTH = 31
SSM_DIM = 512
SSM_GROUP = 16
SSM_GROUPS = SSM_DIM // SSM_GROUP
SSM_STATE = 64
NA_HEADS = 16
NA_HEAD_DIM = 64
NA_WIN_ROWS = 8
NA_WIN_COLS = 16
NORM_EPS = 1e-6
LN_EPS = 1e-5
MASK_VALUE = -1e30

S5_CHUNK = 64
S5_STEP_BLOCK = 8
S5_CHUNK_BLOCK = 128
V7X_LANES = 128
V7X_SUBLANES = 8
V7X_VMEM_LIMIT = 56 * 1024 * 1024
ROW_TILE = 512
CONV_TILE = 256
CONV_PAD = 16
NA_ROWS_PER_STEP = 4


def _params(n_axes):
    return pltpu.CompilerParams(dimension_semantics=("arbitrary",) * n_axes,
                                vmem_limit_bytes=V7X_VMEM_LIMIT)


def _resident(shape):
    return pl.BlockSpec(shape, lambda *_: (0,) * len(shape), pipeline_mode=pl.Buffered(1))


def _rms_norm(x, gain):
    ms = jnp.mean(x * x, axis=-1, keepdims=True)
    return x * lax.rsqrt(ms + NORM_EPS) * gain


def _dot(a, b):
    return jnp.dot(a, b, preferred_element_type=F32)


def _dot_nt(a, b):
    return lax.dot_general(a, b, (((1,), (1,)), ((), ())), preferred_element_type=F32)


def _norm_matmul_body(x_ref, g_ref, w_ref, o_ref):
    h = _rms_norm(x_ref[...], g_ref[...]).astype(BF16)
    o_ref[...] = _dot(h, w_ref[...]).astype(o_ref.dtype)


def _norm_matmul(x, gain, w):
    m, d = x.shape
    n = w.shape[1]
    return pl.pallas_call(
        _norm_matmul_body,
        grid=(m // ROW_TILE,),
        in_specs=[pl.BlockSpec((ROW_TILE, d), lambda i: (i, 0)),
                  _resident((1, d)), _resident((d, n))],
        out_specs=pl.BlockSpec((ROW_TILE, n), lambda i: (i, 0)),
        out_shape=jax.ShapeDtypeStruct((m, n), BF16),
        compiler_params=_params(1),
        name="norm_matmul",
    )(x, gain.reshape(1, d), w)


def _ffn_tail(x1, fg_ref, wg_ref, wu_ref, wd_ref):
    h = _rms_norm(x1, fg_ref[...]).astype(BF16)
    g = _dot(h, wg_ref[...])
    u = _dot(h, wu_ref[...])
    a = (g * jax.nn.sigmoid(g) * u).astype(BF16)
    return x1 + _dot(a, wd_ref[...])


def _attn_ffn_body(x_ref, a_ref, wo_ref, fg_ref, wg_ref, wu_ref, wd_ref, o_ref):
    x1 = x_ref[...] + _dot(a_ref[...], wo_ref[...])
    o_ref[...] = _ffn_tail(x1, fg_ref, wg_ref, wu_ref, wd_ref)


def _ssm_ffn_body(x_ref, ya_ref, ys_ref, u_ref, d_ref, gw_ref, gb_ref, wo_ref,
                  fg_ref, wg_ref, wu_ref, wd_ref, o_ref):
    y = ys_ref[...] + d_ref[...] * u_ref[...]
    y = jax.nn.gelu(y, approximate=True)
    gate = jax.nn.sigmoid(_dot(y.astype(BF16), gw_ref[...]) + gb_ref[...])
    yb = (y * gate).astype(BF16)
    x1 = (x_ref[...] + _dot(ya_ref[...], wo_ref[:CONV_DIM, :]) + _dot(yb, wo_ref[CONV_DIM:, :]))
    o_ref[...] = _ffn_tail(x1, fg_ref, wg_ref, wu_ref, wd_ref)


def _row_spec(width, col_block=0):
    return pl.BlockSpec((ROW_TILE, width), lambda i: (i, col_block))


def _attn_ffn(x, attn, w_out, ffn_gain, w_gate, w_up, w_down):
    m, d = x.shape
    f = w_gate.shape[1]
    return pl.pallas_call(
        _attn_ffn_body,
        grid=(m // ROW_TILE,),
        in_specs=[_row_spec(d), _row_spec(d), _resident((d, d)), _resident((1, d)),
                  _resident((d, f)), _resident((d, f)), _resident((f, d))],
        out_specs=_row_spec(d),
        out_shape=jax.ShapeDtypeStruct((m, d), F32),
        compiler_params=_params(1),
        name="attn_out_ffn",
    )(x, attn, w_out, ffn_gain.reshape(1, d), w_gate, w_up, w_down)


def _ssm_ffn(x, ya, ys, z, d_skip, glu_w, glu_b, w_out, ffn_gain, w_gate, w_up, w_down):
    m, d = x.shape
    f = w_gate.shape[1]
    c = SSM_DIM
    return pl.pallas_call(
        _ssm_ffn_body,
        grid=(m // ROW_TILE,),
        in_specs=[_row_spec(d), _row_spec(CONV_DIM), _row_spec(c),
                  _row_spec(c, col_block=2 * CONV_DIM // c),
                  _resident((1, c)), _resident((c, c)), _resident((1, c)),
                  _resident((CONV_DIM + c, d)), _resident((1, d)),
                  _resident((d, f)), _resident((d, f)), _resident((f, d))],
        out_specs=_row_spec(d),
        out_shape=jax.ShapeDtypeStruct((m, d), F32),
        compiler_params=_params(1),
        name="ssm_out_ffn",
    )(x, ya, ys, z, d_skip.reshape(1, c), glu_w, glu_b.reshape(1, c), w_out,
      ffn_gain.reshape(1, d), w_gate, w_up, w_down)


def _token_block_spec(width):
    return pl.BlockSpec((S5_CHUNK_BLOCK, S5_STEP_BLOCK, width), lambda a, cb: (cb, a, 0))


def _slab_block_spec():
    return pl.BlockSpec((SSM_GROUPS, S5_STEP_BLOCK, SSM_GROUP, S5_CHUNK_BLOCK),
                        lambda a, cb: (0, a, 0, cb))


def _ssm_in_proj_body(x_ref, g_ref, w_ref, z_ref, ut_ref, u_scr):
    cb, st, d = x_ref.shape
    h = _rms_norm(x_ref[...].reshape(cb * st, d), g_ref[...]).astype(BF16)
    z = _dot(h, w_ref[...])
    z_ref[...] = z.reshape(cb, st, -1)
    lanes = V7X_LANES
    groups = lanes // SSM_GROUP
    for l in range(SSM_DIM // lanes):
        u_scr[l] = z[:, 2 * CONV_DIM + l * lanes:2 * CONV_DIM + (l + 1) * lanes]
        for t in range(st):
            ut = u_scr[l, pl.ds(t, cb, stride=st), :].T
            ut_ref[l * groups:(l + 1) * groups, t] = (
                ut.reshape(groups, SSM_GROUP, cb).astype(ut_ref.dtype))


def _ssm_in_proj(xc, gain, w):
    rows, chunk, d = xc.shape
    n = w.shape[1]
    assert rows % S5_CHUNK_BLOCK == 0 and chunk % S5_STEP_BLOCK == 0
    return pl.pallas_call(
        _ssm_in_proj_body,
        grid=(chunk // S5_STEP_BLOCK, rows // S5_CHUNK_BLOCK),
        in_specs=[_token_block_spec(d), _resident((1, d)), _resident((d, n))],
        out_specs=[_token_block_spec(n), _slab_block_spec()],
        out_shape=[jax.ShapeDtypeStruct((rows, chunk, n), F32),
                   jax.ShapeDtypeStruct((SSM_GROUPS, chunk, SSM_GROUP, rows), BF16)],
        scratch_shapes=[pltpu.VMEM((SSM_DIM // V7X_LANES, S5_CHUNK_BLOCK * S5_STEP_BLOCK, V7X_LANES),
                                   F32)],
        compiler_params=_params(2),
        name="ssm_in_proj",
    )(xc, gain.reshape(1, d), w)


def _s5_unlayout_body(yt_ref, ys_ref, y_scr):
    g, st, p, cb = yt_ref.shape
    lanes = V7X_LANES
    groups = lanes // p
    for l in range(g * p // lanes):
        for t in range(st):
            y = yt_ref[l * groups:(l + 1) * groups, t].astype(F32).reshape(lanes, cb).T
            y_scr[l, pl.ds(t, cb, stride=st), :] = y
        ys_ref[:, :, l * lanes:(l + 1) * lanes] = y_scr[l].reshape(cb, st, lanes)


def _s5_unlayout(yt):
    g, chunk, p, rows = yt.shape
    return pl.pallas_call(
        _s5_unlayout_body,
        grid=(chunk // S5_STEP_BLOCK, rows // S5_CHUNK_BLOCK),
        in_specs=[_slab_block_spec()],
        out_specs=_token_block_spec(g * p),
        out_shape=jax.ShapeDtypeStruct((rows, chunk, g * p), F32),
        scratch_shapes=[pltpu.VMEM((g * p // V7X_LANES, S5_CHUNK_BLOCK * S5_STEP_BLOCK, V7X_LANES),
                                   F32)],
        compiler_params=_params(2),
        name="s5_unlayout",
    )(yt)


def _conv_body(za_ref, zg_ref, w_ref, cb_ref, lg_ref, lb_ref, o_ref, apad_ref, *, seq):
    c = CONV_DIM
    apad_ref[0:CONV_PAD, :] = jnp.zeros((CONV_PAD, c), F32)
    apad_ref[CONV_PAD + seq:, :] = jnp.zeros((CONV_PAD, c), F32)

    glu_tile = 256

    def glu_step(i, carry):
        r0 = pl.multiple_of(i * glu_tile, glu_tile)
        za = za_ref[0, pl.ds(r0, glu_tile), :].astype(F32)
        zg = zg_ref[0, pl.ds(r0, glu_tile), :].astype(F32)
        apad_ref[pl.ds(CONV_PAD + r0, glu_tile), :] = za * jax.nn.sigmoid(zg)
        return carry

    lax.fori_loop(0, seq // glu_tile, glu_step, 0)

    first = CONV_PAD - CONV_WIDTH // 2
    sub = V7X_SUBLANES

    def conv_step(i, carry):
        r0 = pl.multiple_of(i * CONV_TILE, CONV_TILE)
        win = apad_ref[pl.ds(r0, CONV_TILE + 2 * CONV_PAD), :]
        acc = cb_ref[...]
        for r in range(sub):
            part = None
            for k in range(r, CONV_WIDTH, sub):
                term = w_ref[k:k + 1, :] * win[k - r:k - r + CONV_TILE + sub]
                part = term if part is None else part + term
            acc = acc + part[first + r:first + r + CONV_TILE]
        mu = jnp.mean(acc, axis=-1, keepdims=True)
        xc = acc - mu
        var = jnp.mean(xc * xc, axis=-1, keepdims=True)
        y = xc * lax.rsqrt(var + LN_EPS) * lg_ref[...] + lb_ref[...]
        o_ref[0, pl.ds(r0, CONV_TILE), :] = (y * jax.nn.sigmoid(y)).astype(o_ref.dtype)
        return carry

    lax.fori_loop(0, seq // CONV_TILE, conv_step, 0)


def _conv_module(z3, conv_w, conv_b, ln_g, ln_b):
    bsz, seq, _ = z3.shape
    c = CONV_DIM
    return pl.pallas_call(
        functools.partial(_conv_body, seq=seq),
        grid=(bsz,),
        in_specs=[pl.BlockSpec((1, seq, c), lambda b: (b, 0, 0)),
                  pl.BlockSpec((1, seq, c), lambda b: (b, 0, 1)),
                  _resident((CONV_WIDTH, c)), _resident((1, c)), _resident((1, c)),
                  _resident((1, c))],
        out_specs=pl.BlockSpec((1, seq, c), lambda b: (b, 0, 0)),
        out_shape=jax.ShapeDtypeStruct((bsz, seq, c), BF16),
        scratch_shapes=[pltpu.VMEM((seq + 2 * CONV_PAD, c), F32)],
        compiler_params=_params(1),
        name="conformer_conv",
    )(z3, z3, conv_w, conv_b.reshape(1, c), ln_g.reshape(1, c), ln_b.reshape(1, c))


def _s5_body(xt_ref, par_ref, btr_ref, bti_ref, cr_ref, ci_ref, yt_ref, t_ref, w8_ref,
             *, chunk, n_chunks):
    n = SSM_STATE
    n2 = 2 * n
    p = SSM_GROUP
    r = p * chunk
    fwd = lax.broadcasted_iota(jnp.int32, (1, n2), 1) < n
    step = lax.broadcasted_iota(jnp.int32, (chunk, 1), 0).astype(F32)

    def split(a, terms):
        out = []
        for _ in range(terms):
            hi = a.astype(BF16)
            out.append(hi)
            a = a - hi.astype(F32)
        return out

    def expand(a):
        return jnp.concatenate(
            [jnp.broadcast_to(a[j:j + 1, :], (p, a.shape[1])) for j in range(chunk)], axis=0)

    def tile_rows(a):
        return jnp.concatenate([a] * chunk, axis=0)

    lr = par_ref[0, 0:1, :]
    li = par_ref[0, 1:2, :]
    dt = jnp.exp(par_ref[0, 2:3, :])
    lr_dt = lr * dt
    li_dt = li * dt

    def cpow(e):
        mag = jnp.exp(lr_dt * e)
        ang = li_dt * e
        return mag * jnp.cos(ang), mag * jnp.sin(ang)

    ar, ai = cpow(1.0)
    den = lr * lr + li * li
    pr = ar - 1.0
    coef_re = (pr * lr + ai * li) / den
    coef_im = (ai * lr - pr * li) / den
    btr = btr_ref[0]
    bti = bti_ref[0]
    bbr = tile_rows(coef_re * btr - coef_im * bti)
    bbi = tile_rows(coef_re * bti + coef_im * btr)
    ctr = tile_rows(cr_ref[0])
    cti = tile_rows(ci_ref[0])
    e_m = jnp.where(fwd, (chunk - 1.0) - step, step)
    e_q = jnp.where(fwd, step + 1.0, chunk - step)
    powers = expand(jnp.concatenate(cpow(e_m) + cpow(e_q), axis=1))
    pmr, pmi, pqr, pqi = (powers[:, i * n2:(i + 1) * n2] for i in range(4))
    mr = bbr * pmr - bbi * pmi
    mi = bbr * pmi + bbi * pmr
    qr = ctr * pqr - cti * pqi
    qi = ctr * pqi + cti * pqr

    def strip_rows(v):
        zero = jnp.zeros_like(v)
        f = jnp.where(fwd, v, zero)
        b = jnp.where(fwd, zero, v)
        return (jnp.concatenate([f, zero], axis=0)
                + jnp.concatenate([zero[:r - p], b, zero[:p]], axis=0))

    def dot_nt_split(a, b_terms):
        a_hi, a_lo = split(a, 2)
        b_hi, b_lo = b_terms
        return _dot_nt(a_hi, b_hi) + _dot_nt(a_hi, b_lo) + _dot_nt(a_lo, b_hi)

    mr_terms = split(mr, 2)
    mi_terms = split(mi, 2)
    wc = (dot_nt_split(cr_ref[0], [strip_rows(v) for v in mr_terms])
          - dot_nt_split(ci_ref[0], [strip_rows(v) for v in mi_terms]))

    width = 2 * r - V7X_LANES
    for s in range(V7X_LANES // p):
        w8_ref[s] = wc[:, p * s:p * s + width]
    for t in range(chunk):
        a, s = divmod(p * (chunk - 1 - t), V7X_LANES)
        t_ref[p * t:p * (t + 1), :] = w8_ref[s // p, :, V7X_LANES * a:V7X_LANES * a + r].astype(BF16)

    cols = xt_ref.shape[-1]
    xt = xt_ref[0].reshape(r, cols)
    y = _dot(t_ref[...], xt)

    e = _dot(jnp.concatenate([mr, mi], axis=1).T.astype(BF16), xt)
    lvl = lax.broadcasted_iota(jnp.int32, (V7X_SUBLANES, 1), 0)
    lre, lim = cpow((chunk * jnp.left_shift(1, lvl)).astype(F32))
    pcol = jnp.concatenate([lre, lim, jnp.zeros((n2 - 2 * V7X_SUBLANES, n2), F32)], axis=0).T
    jl = lax.broadcasted_iota(jnp.int32, (1, cols), 1) % n_chunks

    prev = {}
    for d in range(2):
        rows_d = slice(d * n, (d + 1) * n)
        sr, si = e[rows_d], e[n2 + d * n:n2 + (d + 1) * n]
        shifted = lambda v, s: pltpu.roll(v, s if d == 0 else cols - s, axis=1)
        k, s = 0, 1
        while s < n_chunks:
            keep = (jl >= s) if d == 0 else (jl < n_chunks - s)
            a_r = pcol[rows_d, k:k + 1]
            a_i = pcol[rows_d, V7X_SUBLANES + k:V7X_SUBLANES + k + 1]
            rr, ri = shifted(sr, s), shifted(si, s)
            sr, si = (sr + jnp.where(keep, a_r * rr - a_i * ri, 0.0),
                      si + jnp.where(keep, a_r * ri + a_i * rr, 0.0))
            k, s = k + 1, 2 * s
        keep = (jl >= 1) if d == 0 else (jl < n_chunks - 1)
        prev[d] = (jnp.where(keep, shifted(sr, 1), 0.0), jnp.where(keep, shifted(si, 1), 0.0))

    s_prev = jnp.concatenate([prev[0][0], prev[1][0], prev[0][1], prev[1][1]], axis=0)
    q_cat = jnp.concatenate([qr, -qi], axis=1)
    y = y + _dot(q_cat.astype(BF16), s_prev.astype(BF16))
    yt_ref[0] = y.reshape(chunk, p, cols).astype(yt_ref.dtype)


def _s5_scan(ut, lam_re, lam_im, log_step, b_re, b_im, c_re, c_im, n_chunks):
    g, chunk, p, cols = ut.shape
    n2 = 2 * SSM_STATE
    assert cols % V7X_LANES == 0 and n_chunks <= 2 ** V7X_SUBLANES
    r = p * chunk
    both = lambda a: a.transpose(1, 2, 0, 3).reshape(g, -1, n2)
    par = both(jnp.stack([lam_re, lam_im, jnp.broadcast_to(log_step[..., None], lam_re.shape)],
                         axis=2))
    bt_re = both(b_re.transpose(0, 1, 3, 2))
    bt_im = both(b_im.transpose(0, 1, 3, 2))
    grp = lambda *tail: pl.BlockSpec((1,) + tail, lambda i: (i,) + (0,) * len(tail))
    return pl.pallas_call(
        functools.partial(_s5_body, chunk=chunk, n_chunks=n_chunks),
        grid=(g,),
        in_specs=[grp(chunk, p, cols), grp(3, n2), grp(p, n2), grp(p, n2), grp(p, n2), grp(p, n2)],
        out_specs=grp(chunk, p, cols),
        out_shape=jax.ShapeDtypeStruct(ut.shape, BF16),
        scratch_shapes=[pltpu.VMEM((r, r), BF16),
                        pltpu.VMEM((V7X_LANES // p, p, 2 * r - V7X_LANES), F32)],
        compiler_params=_params(1),
        name="s5_chunked",
    )(ut, par, bt_re, bt_im, both(c_re), both(c_im))


def _na_body(q_ref, k_ref, v_ref, qg_ref, kg_ref, tab_ref, mask_ref, o_ref,
             q0_ref, q1_ref, kn_ref, sa_ref, sb_ref, *, rows):
    w = GRID_W
    keys = NA_WIN_ROWS * w
    pair = 2 * NA_HEAD_DIM
    head0 = lax.broadcasted_iota(jnp.int32, (1, pair), 1) < NA_HEAD_DIM
    same_head = (lax.broadcasted_iota(jnp.int32, (pair, pair), 0) // NA_HEAD_DIM
                 == lax.broadcasted_iota(jnp.int32, (pair, pair), 1) // NA_HEAD_DIM).astype(BF16)

    def head_norm(t, gain):
        sq = t * t
        hi = sq.astype(BF16)
        lo = (sq - hi.astype(F32)).astype(BF16)
        ms = (_dot(hi, same_head) + _dot(lo, same_head)) / NA_HEAD_DIM
        return t * lax.rsqrt(ms + NORM_EPS) * gain

    qn = head_norm(q_ref[0].astype(F32), qg_ref[...]) * (NA_HEAD_DIM ** -0.5)
    q0_ref[...] = jnp.where(head0, qn, 0.0).astype(BF16)
    q1_ref[...] = jnp.where(head0, 0.0, qn).astype(BF16)
    kn_ref[...] = head_norm(k_ref[0].astype(F32), kg_ref[...]).astype(BF16)

    n_steps = rows // NA_ROWS_PER_STEP

    def offsets(step, j):
        row = step * NA_ROWS_PER_STEP + j
        start = jnp.clip(row - NA_WIN_ROWS // 2, 0, rows - NA_WIN_ROWS)
        return (pl.multiple_of(row * w, w), pl.multiple_of(start * w, w),
                start - row + (NA_WIN_ROWS - 1))

    def score_row(step, j, stage_ref):
        q_off, k_off, _ = offsets(step, j)
        kblk = kn_ref[pl.ds(k_off, keys), :]
        for h, qh_ref in enumerate((q0_ref, q1_ref)):
            stage_ref[j, h] = _dot_nt(qh_ref[pl.ds(q_off, w), :], kblk)

    def value_row(step, j, stage_ref):
        q_off, k_off, bias_row = offsets(step, j)
        vblk = v_ref[0, pl.ds(k_off, keys), :]
        valid = mask_ref[...] != 0
        outs = []
        for h in range(2):
            s = jnp.where(valid, stage_ref[j, h] + tab_ref[0, h, bias_row], MASK_VALUE)
            e = jnp.exp(s - jnp.max(s, axis=-1, keepdims=True))
            outs.append(_dot(e.astype(BF16), vblk) / jnp.sum(e, axis=-1, keepdims=True))
        o_ref[0, pl.ds(q_off, w), :] = jnp.where(head0, outs[0], outs[1]).astype(o_ref.dtype)

    for j in range(NA_ROWS_PER_STEP):
        score_row(0, j, sa_ref)

    def step_pair(i, carry):
        even = 2 * i
        for j in range(NA_ROWS_PER_STEP):
            score_row(even + 1, j, sb_ref)
            value_row(even, j, sa_ref)
        nxt = jnp.minimum(even + 2, n_steps - 1)
        for j in range(NA_ROWS_PER_STEP):
            score_row(nxt, j, sa_ref)
            value_row(even + 1, j, sb_ref)
        return carry

    lax.fori_loop(0, n_steps // 2, step_pair, 0)


def _na_bias_tables(rpb):
    w = GRID_W
    cols = np.arange(w)
    col_idx = np.clip(cols[None, :] - cols[:, None], 1 - NA_WIN_COLS, NA_WIN_COLS - 1) + (NA_WIN_COLS - 1)
    cb = rpb[:, :, col_idx]
    idx = np.arange(NA_WIN_ROWS)[:, None] + np.arange(NA_WIN_ROWS)[None, :]
    t = cb[:, idx]
    t = t.transpose(0, 1, 3, 2, 4).reshape(NA_HEADS // 2, 2, NA_WIN_ROWS, w, NA_WIN_ROWS * w)
    return t


def _na_col_mask():
    w = GRID_W
    cols = np.arange(w)
    col_start = np.clip(cols - NA_WIN_COLS // 2, 0, w - NA_WIN_COLS)
    m = (cols[None, :] >= col_start[:, None]) & (cols[None, :] < col_start[:, None] + NA_WIN_COLS)
    return jnp.asarray(np.tile(m, (1, NA_WIN_ROWS)).astype(np.int32))


def _neighbourhood_attention(qkv3, q_gain, k_gain, rpb):
    bsz, seq, d3 = qkv3.shape
    d = d3 // 3
    rows = seq // GRID_W
    assert rows >= NA_WIN_ROWS and rows % (2 * NA_ROWS_PER_STEP) == 0 and d == NA_HEADS * NA_HEAD_DIM
    pair = 2 * NA_HEAD_DIM
    n_pairs = d // pair
    keys = NA_WIN_ROWS * GRID_W
    blk = lambda off: pl.BlockSpec((1, seq, pair), lambda hp, b: (b, 0, off + hp))
    gains = lambda gvec: jnp.tile(gvec, 2).reshape(1, pair)
    return pl.pallas_call(
        functools.partial(_na_body, rows=rows),
        grid=(n_pairs, bsz),
        in_specs=[blk(0), blk(n_pairs), blk(2 * n_pairs), _resident((1, pair)), _resident((1, pair)),
                  pl.BlockSpec((1, 2, NA_WIN_ROWS, GRID_W, keys), lambda hp, b: (hp, 0, 0, 0, 0)),
                  _resident((GRID_W, keys))],
        out_specs=pl.BlockSpec((1, seq, pair), lambda hp, b: (b, 0, hp)),
        out_shape=jax.ShapeDtypeStruct((bsz, seq, d), BF16),
        scratch_shapes=[pltpu.VMEM((seq, pair), BF16)] * 3
        + [pltpu.VMEM((NA_ROWS_PER_STEP, 2, GRID_W, keys), F32)] * 2,
        compiler_params=_params(2),
        name="neighbourhood_attention",
    )(qkv3, qkv3, qkv3, gains(q_gain), gains(k_gain), _na_bias_tables(rpb), _na_col_mask())


def kernel(x, mix_norm, ffn_norm, ffn_w_gate, ffn_w_up, ffn_w_down, ab_w_in, conv_w, conv_b, conv_ln_g, conv_ln_b, ssm_lambda_re, ssm_lambda_im, ssm_log_step, ssm_b_re, ssm_b_im, ssm_c_re, ssm_c_im, ssm_d, ssm_glu_w, ssm_glu_b, ab_w_out, na_w_qkv, na_q_norm, na_k_norm, na_rpb, na_w_out):
    bsz, seq, d = x.shape
    m = bsz * seq
    chunk = S5_CHUNK
    assert m % ROW_TILE == 0 and seq % chunk == 0
    n_chunks = seq // chunk
    rows_c = bsz * n_chunks
    depth = mix_norm.shape[0]
    bf = lambda w: w.astype(BF16)
    xs = x.reshape(m, d)
    for layer in range(depth):
        i = layer // 2
        ffn = (ffn_norm[layer], bf(ffn_w_gate[layer]), bf(ffn_w_up[layer]), bf(ffn_w_down[layer]))
        if layer % 2 == 0:
            xc = xs.reshape(rows_c, chunk, d)
            z, ut = _ssm_in_proj(xc, mix_norm[layer], bf(ab_w_in[i]))
            ya = _conv_module(z.reshape(bsz, seq, -1), conv_w[i], conv_b[i], conv_ln_g[i],
                              conv_ln_b[i])
            yt = _s5_scan(ut, ssm_lambda_re[i], ssm_lambda_im[i], ssm_log_step[i], ssm_b_re[i],
                          ssm_b_im[i], ssm_c_re[i], ssm_c_im[i], n_chunks)
            ys = _s5_unlayout(yt)
            xs = _ssm_ffn(xs, ya.reshape(m, -1), ys.reshape(m, -1), z.reshape(m, -1), ssm_d[i],
                          bf(ssm_glu_w[i]), ssm_glu_b[i], bf(ab_w_out[i]), *ffn)
        else:
            qkv = _norm_matmul(xs, mix_norm[layer], bf(na_w_qkv[i]))
            attn = _neighbourhood_attention(qkv.reshape(bsz, seq, -1), na_q_norm[i], na_k_norm[i],
                                            na_rpb[i])
            xs = _attn_ffn(xs, attn.reshape(m, d), bf(na_w_out[i]), *ffn)
    return xs.reshape(bsz, seq, d)
```

```python
import functools

import numpy as np
import jax
import jax.numpy as jnp
from jax import lax
from jax.experimental import pallas as pl
from jax.experimental.pallas import tpu as pltpu

F32 = jnp.float32
BF16 = jnp.bfloat16

GRID_W = 64
CONV_DIM = 512
CONV_WIDTH = 31
SSM_DIM = 512
SSM_GROUP = 16
SSM_GROUPS = SSM_DIM // SSM_GROUP
SSM_STATE = 64
NA_HEADS = 16
NA_HEAD_DIM = 64
NA_WIN_ROWS = 8
NA_WIN_COLS = 16
NORM_EPS = 1e-6
LN_EPS = 1e-5
MASK_VALUE = -1e30

S5_CHUNK = 64
S5_STEP_BLOCK = 8
S5_CHUNK_BLOCK = 128
V7X_LANES = 128
V7X_SUBLANES = 8
V7X_VMEM_LIMIT = 56 * 1024 * 1024
ROW_TILE = 512
CONV_TILE = 256
CONV_PAD = 16
NA_ROWS_PER_STEP = 4


def _params(n_axes):
    return pltpu.CompilerParams(dimension_semantics=("arbitrary",) * n_axes,
                                vmem_limit_bytes=V7X_VMEM_LIMIT)


def _resident(shape):
    return pl.BlockSpec(shape, lambda *_: (0,) * len(shape), pipeline_mode=pl.Buffered(1))


def _rms_norm(x, gain):
    ms = jnp.mean(x * x, axis=-1, keepdims=True)
    return x * lax.rsqrt(ms + NORM_EPS) * gain


def _dot(a, b):
    return jnp.dot(a, b, preferred_element_type=F32)


def _dot_nt(a, b):
    return lax.dot_general(a, b, (((1,), (1,)), ((), ())), preferred_element_type=F32)


def _norm_matmul_body(x_ref, g_ref, w_ref, o_ref):
    h = _rms_norm(x_ref[...], g_ref[...]).astype(BF16)
    o_ref[...] = _dot(h, w_ref[...]).astype(o_ref.dtype)


def _norm_matmul(x, gain, w):
    m, d = x.shape
    n = w.shape[1]
    return pl.pallas_call(
        _norm_matmul_body,
        grid=(m // ROW_TILE,),
        in_specs=[pl.BlockSpec((ROW_TILE, d), lambda i: (i, 0)),
                  _resident((1, d)), _resident((d, n))],
        out_specs=pl.BlockSpec((ROW_TILE, n), lambda i: (i, 0)),
        out_shape=jax.ShapeDtypeStruct((m, n), BF16),
        compiler_params=_params(1),
        name="norm_matmul",
    )(x, gain.reshape(1, d), w)


def _ffn_tail(x1, fg_ref, wg_ref, wu_ref, wd_ref):
    h = _rms_norm(x1, fg_ref[...]).astype(BF16)
    g = _dot(h, wg_ref[...])
    u = _dot(h, wu_ref[...])
    a = (g * jax.nn.sigmoid(g) * u).astype(BF16)
    return x1 + _dot(a, wd_ref[...])


def _attn_ffn_body(x_ref, a_ref, wo_ref, fg_ref, wg_ref, wu_ref, wd_ref, o_ref):
    x1 = x_ref[...] + _dot(a_ref[...], wo_ref[...])
    o_ref[...] = _ffn_tail(x1, fg_ref, wg_ref, wu_ref, wd_ref)


def _ssm_ffn_body(x_ref, ya_ref, ys_ref, u_ref, d_ref, gw_ref, gb_ref, wo_ref,
                  fg_ref, wg_ref, wu_ref, wd_ref, o_ref):
    y = ys_ref[...] + d_ref[...] * u_ref[...]
    y = jax.nn.gelu(y, approximate=True)
    gate = jax.nn.sigmoid(_dot(y.astype(BF16), gw_ref[...]) + gb_ref[...])
    yb = (y * gate).astype(BF16)
    x1 = (x_ref[...] + _dot(ya_ref[...], wo_ref[:CONV_DIM, :]) + _dot(yb, wo_ref[CONV_DIM:, :]))
    o_ref[...] = _ffn_tail(x1, fg_ref, wg_ref, wu_ref, wd_ref)


def _row_spec(width, col_block=0):
    return pl.BlockSpec((ROW_TILE, width), lambda i: (i, col_block))


def _attn_ffn(x, attn, w_out, ffn_gain, w_gate, w_up, w_down):
    m, d = x.shape
    f = w_gate.shape[1]
    return pl.pallas_call(
        _attn_ffn_body,
        grid=(m // ROW_TILE,),
        in_specs=[_row_spec(d), _row_spec(d), _resident((d, d)), _resident((1, d)),
                  _resident((d, f)), _resident((d, f)), _resident((f, d))],
        out_specs=_row_spec(d),
        out_shape=jax.ShapeDtypeStruct((m, d), F32),
        compiler_params=_params(1),
        name="attn_out_ffn",
    )(x, attn, w_out, ffn_gain.reshape(1, d), w_gate, w_up, w_down)


def _ssm_ffn(x, ya, ys, z, d_skip, glu_w, glu_b, w_out, ffn_gain, w_gate, w_up, w_down):
    m, d = x.shape
    f = w_gate.shape[1]
    c = SSM_DIM
    return pl.pallas_call(
        _ssm_ffn_body,
        grid=(m // ROW_TILE,),
        in_specs=[_row_spec(d), _row_spec(CONV_DIM), _row_spec(c),
                  _row_spec(c, col_block=2 * CONV_DIM // c),
                  _resident((1, c)), _resident((c, c)), _resident((1, c)),
                  _resident((CONV_DIM + c, d)), _resident((1, d)),
                  _resident((d, f)), _resident((d, f)), _resident((f, d))],
        out_specs=_row_spec(d),
        out_shape=jax.ShapeDtypeStruct((m, d), F32),
        compiler_params=_params(1),
        name="ssm_out_ffn",
    )(x, ya, ys, z, d_skip.reshape(1, c), glu_w, glu_b.reshape(1, c), w_out,
      ffn_gain.reshape(1, d), w_gate, w_up, w_down)


def _token_block_spec(width):
    return pl.BlockSpec((S5_CHUNK_BLOCK, S5_STEP_BLOCK, width), lambda a, cb: (cb, a, 0))


def _slab_block_spec():
    return pl.BlockSpec((SSM_GROUPS, S5_STEP_BLOCK, SSM_GROUP, S5_CHUNK_BLOCK),
                        lambda a, cb: (0, a, 0, cb))


def _ssm_in_proj_body(x_ref, g_ref, w_ref, z_ref, ut_ref, u_scr):
    cb, st, d = x_ref.shape
    h = _rms_norm(x_ref[...].reshape(cb * st, d), g_ref[...]).astype(BF16)
    z = _dot(h, w_ref[...])
    z_ref[...] = z.reshape(cb, st, -1)
    lanes = V7X_LANES
    groups = lanes // SSM_GROUP
    for l in range(SSM_DIM // lanes):
        u_scr[l] = z[:, 2 * CONV_DIM + l * lanes:2 * CONV_DIM + (l + 1) * lanes]
        for t in range(st):
            ut = u_scr[l, pl.ds(t, cb, stride=st), :].T
            ut_ref[l * groups:(l + 1) * groups, t] = (
                ut.reshape(groups, SSM_GROUP, cb).astype(ut_ref.dtype))


def _ssm_in_proj(xc, gain, w):
    rows, chunk, d = xc.shape
    n = w.shape[1]
    assert rows % S5_CHUNK_BLOCK == 0 and chunk % S5_STEP_BLOCK == 0
    return pl.pallas_call(
        _ssm_in_proj_body,
        grid=(chunk // S5_STEP_BLOCK, rows // S5_CHUNK_BLOCK),
        in_specs=[_token_block_spec(d), _resident((1, d)), _resident((d, n))],
        out_specs=[_token_block_spec(n), _slab_block_spec()],
        out_shape=[jax.ShapeDtypeStruct((rows, chunk, n), F32),
                   jax.ShapeDtypeStruct((SSM_GROUPS, chunk, SSM_GROUP, rows), BF16)],
        scratch_shapes=[pltpu.VMEM((SSM_DIM // V7X_LANES, S5_CHUNK_BLOCK * S5_STEP_BLOCK, V7X_LANES),
                                   F32)],
        compiler_params=_params(2),
        name="ssm_in_proj",
    )(xc, gain.reshape(1, d), w)


def _s5_unlayout_body(yt_ref, ys_ref, y_scr):
    g, st, p, cb = yt_ref.shape
    lanes = V7X_LANES
    groups = lanes // p
    for l in range(g * p // lanes):
        for t in range(st):
            y = yt_ref[l * groups:(l + 1) * groups, t].astype(F32).reshape(lanes, cb).T
            y_scr[l, pl.ds(t, cb, stride=st), :] = y
        ys_ref[:, :, l * lanes:(l + 1) * lanes] = y_scr[l].reshape(cb, st, lanes)


def _s5_unlayout(yt):
    g, chunk, p, rows = yt.shape
    return pl.pallas_call(
        _s5_unlayout_body,
        grid=(chunk // S5_STEP_BLOCK, rows // S5_CHUNK_BLOCK),
        in_specs=[_slab_block_spec()],
        out_specs=_token_block_spec(g * p),
        out_shape=jax.ShapeDtypeStruct((rows, chunk, g * p), F32),
        scratch_shapes=[pltpu.VMEM((g * p // V7X_LANES, S5_CHUNK_BLOCK * S5_STEP_BLOCK, V7X_LANES),
                                   F32)],
        compiler_params=_params(2),
        name="s5_unlayout",
    )(yt)


def _conv_body(za_ref, zg_ref, w_ref, cb_ref, lg_ref, lb_ref, o_ref, apad_ref, *, seq):
    c = CONV_DIM
    apad_ref[0:CONV_PAD, :] = jnp.zeros((CONV_PAD, c), F32)
    apad_ref[CONV_PAD + seq:, :] = jnp.zeros((CONV_PAD, c), F32)

    glu_tile = 256

    def glu_step(i, carry):
        r0 = pl.multiple_of(i * glu_tile, glu_tile)
        za = za_ref[0, pl.ds(r0, glu_tile), :].astype(F32)
        zg = zg_ref[0, pl.ds(r0, glu_tile), :].astype(F32)
        apad_ref[pl.ds(CONV_PAD + r0, glu_tile), :] = za * jax.nn.sigmoid(zg)
        return carry

    lax.fori_loop(0, seq // glu_tile, glu_step, 0)

    first = CONV_PAD - CONV_WIDTH // 2
    sub = V7X_SUBLANES

    def conv_step(i, carry):
        r0 = pl.multiple_of(i * CONV_TILE, CONV_TILE)
        win = apad_ref[pl.ds(r0, CONV_TILE + 2 * CONV_PAD), :]
        acc = cb_ref[...]
        for r in range(sub):
            part = None
            for k in range(r, CONV_WIDTH, sub):
                term = w_ref[k:k + 1, :] * win[k - r:k - r + CONV_TILE + sub]
                part = term if part is None else part + term
            acc = acc + part[first + r:first + r + CONV_TILE]
        mu = jnp.mean(acc, axis=-1, keepdims=True)
        xc = acc - mu
        var = jnp.mean(xc * xc, axis=-1, keepdims=True)
        y = xc * lax.rsqrt(var + LN_EPS) * lg_ref[...] + lb_ref[...]
        o_ref[0, pl.ds(r0, CONV_TILE), :] = (y * jax.nn.sigmoid(y)).astype(o_ref.dtype)
        return carry

    lax.fori_loop(0, seq // CONV_TILE, conv_step, 0)


def _conv_module(z3, conv_w, conv_b, ln_g, ln_b):
    bsz, seq, _ = z3.shape
    c = CONV_DIM
    return pl.pallas_call(
        functools.partial(_conv_body, seq=seq),
        grid=(bsz,),
        in_specs=[pl.BlockSpec((1, seq, c), lambda b: (b, 0, 0)),
                  pl.BlockSpec((1, seq, c), lambda b: (b, 0, 1)),
                  _resident((CONV_WIDTH, c)), _resident((1, c)), _resident((1, c)),
                  _resident((1, c))],
        out_specs=pl.BlockSpec((1, seq, c), lambda b: (b, 0, 0)),
        out_shape=jax.ShapeDtypeStruct((bsz, seq, c), BF16),
        scratch_shapes=[pltpu.VMEM((seq + 2 * CONV_PAD, c), F32)],
        compiler_params=_params(1),
        name="conformer_conv",
    )(z3, z3, conv_w, conv_b.reshape(1, c), ln_g.reshape(1, c), ln_b.reshape(1, c))


def _s5_body(xt_ref, par_ref, btr_ref, bti_ref, cr_ref, ci_ref, yt_ref, t_ref, w8_ref,
             *, chunk, n_chunks):
    n = SSM_STATE
    n2 = 2 * n
    p = SSM_GROUP
    r = p * chunk
    fwd = lax.broadcasted_iota(jnp.int32, (1, n2), 1) < n
    step = lax.broadcasted_iota(jnp.int32, (chunk, 1), 0).astype(F32)

    def split(a, terms):
        out = []
        for _ in range(terms):
            hi = a.astype(BF16)
            out.append(hi)
            a = a - hi.astype(F32)
        return out

    def expand(a):
        return jnp.concatenate(
            [jnp.broadcast_to(a[j:j + 1, :], (p, a.shape[1])) for j in range(chunk)], axis=0)

    def tile_rows(a):
        return jnp.concatenate([a] * chunk, axis=0)

    lr = par_ref[0, 0:1, :]
    li = par_ref[0, 1:2, :]
    dt = jnp.exp(par_ref[0, 2:3, :])
    lr_dt = lr * dt
    li_dt = li * dt

    def cpow(e):
        mag = jnp.exp(lr_dt * e)
        ang = li_dt * e
        return mag * jnp.cos(ang), mag * jnp.sin(ang)

    ar, ai = cpow(1.0)
    den = lr * lr + li * li
    pr = ar - 1.0
    coef_re = (pr * lr + ai * li) / den
    coef_im = (ai * lr - pr * li) / den
    btr = btr_ref[0]
    bti = bti_ref[0]
    bbr = tile_rows(coef_re * btr - coef_im * bti)
    bbi = tile_rows(coef_re * bti + coef_im * btr)
    ctr = tile_rows(cr_ref[0])
    cti = tile_rows(ci_ref[0])
    e_m = jnp.where(fwd, (chunk - 1.0) - step, step)
    e_q = jnp.where(fwd, step + 1.0, chunk - step)
    powers = expand(jnp.concatenate(cpow(e_m) + cpow(e_q), axis=1))
    pmr, pmi, pqr, pqi = (powers[:, i * n2:(i + 1) * n2] for i in range(4))
    mr = bbr * pmr - bbi * pmi
    mi = bbr * pmi + bbi * pmr
    qr = ctr * pqr - cti * pqi
    qi = ctr * pqi + cti * pqr

    def strip_rows(v):
        zero = jnp.zeros_like(v)
        f = jnp.where(fwd, v, zero)
        b = jnp.where(fwd, zero, v)
        return (jnp.concatenate([f, zero], axis=0)
                + jnp.concatenate([zero[:r - p], b, zero[:p]], axis=0))

    def dot_nt_split(a, b_terms):
        a_hi, a_lo = split(a, 2)
        b_hi, b_lo = b_terms
        return _dot_nt(a_hi, b_hi) + _dot_nt(a_hi, b_lo) + _dot_nt(a_lo, b_hi)

    mr_terms = split(mr, 2)
    mi_terms = split(mi, 2)
    wc = (dot_nt_split(cr_ref[0], [strip_rows(v) for v in mr_terms])
          - dot_nt_split(ci_ref[0], [strip_rows(v) for v in mi_terms]))

    width = 2 * r - V7X_LANES
    for s in range(V7X_LANES // p):
        w8_ref[s] = wc[:, p * s:p * s + width]
    for t in range(chunk):
        a, s = divmod(p * (chunk - 1 - t), V7X_LANES)
        t_ref[p * t:p * (t + 1), :] = w8_ref[s // p, :, V7X_LANES * a:V7X_LANES * a + r].astype(BF16)

    cols = xt_ref.shape[-1]
    xt = xt_ref[0].reshape(r, cols)
    y = _dot(t_ref[...], xt)

    e = _dot(jnp.concatenate([mr, mi], axis=1).T.astype(BF16), xt)
    lvl = lax.broadcasted_iota(jnp.int32, (V7X_SUBLANES, 1), 0)
    lre, lim = cpow((chunk * jnp.left_shift(1, lvl)).astype(F32))
    pcol = jnp.concatenate([lre, lim, jnp.zeros((n2 - 2 * V7X_SUBLANES, n2), F32)], axis=0).T
    jl = lax.broadcasted_iota(jnp.int32, (1, cols), 1) % n_chunks

    prev = {}
    for d in range(2):
        rows_d = slice(d * n, (d + 1) * n)
        sr, si = e[rows_d], e[n2 + d * n:n2 + (d + 1) * n]
        shifted = lambda v, s: pltpu.roll(v, s if d == 0 else cols - s, axis=1)
        k, s = 0, 1
        while s < n_chunks:
            keep = (jl >= s) if d == 0 else (jl < n_chunks - s)
            a_r = pcol[rows_d, k:k + 1]
            a_i = pcol[rows_d, V7X_SUBLANES + k:V7X_SUBLANES + k + 1]
            rr, ri = shifted(sr, s), shifted(si, s)
            sr, si = (sr + jnp.where(keep, a_r * rr - a_i * ri, 0.0),
                      si + jnp.where(keep, a_r * ri + a_i * rr, 0.0))
            k, s = k + 1, 2 * s
        keep = (jl >= 1) if d == 0 else (jl < n_chunks - 1)
        prev[d] = (jnp.where(keep, shifted(sr, 1), 0.0), jnp.where(keep, shifted(si, 1), 0.0))

    s_prev = jnp.concatenate([prev[0][0], prev[1][0], prev[0][1], prev[1][1]], axis=0)
    q_cat = jnp.concatenate([qr, -qi], axis=1)
    y = y + _dot(q_cat.astype(BF16), s_prev.astype(BF16))
    yt_ref[0] = y.reshape(chunk, p, cols).astype(yt_ref.dtype)


def _s5_scan(ut, lam_re, lam_im, log_step, b_re, b_im, c_re, c_im, n_chunks):
    g, chunk, p, cols = ut.shape
    n2 = 2 * SSM_STATE
    assert cols % V7X_LANES == 0 and n_chunks <= 2 ** V7X_SUBLANES
    r = p * chunk
    both = lambda a: a.transpose(1, 2, 0, 3).reshape(g, -1, n2)
    par = both(jnp.stack([lam_re, lam_im, jnp.broadcast_to(log_step[..., None], lam_re.shape)],
                         axis=2))
    bt_re = both(b_re.transpose(0, 1, 3, 2))
    bt_im = both(b_im.transpose(0, 1, 3, 2))
    grp = lambda *tail: pl.BlockSpec((1,) + tail, lambda i: (i,) + (0,) * len(tail))
    return pl.pallas_call(
        functools.partial(_s5_body, chunk=chunk, n_chunks=n_chunks),
        grid=(g,),
        in_specs=[grp(chunk, p, cols), grp(3, n2), grp(p, n2), grp(p, n2), grp(p, n2), grp(p, n2)],
        out_specs=grp(chunk, p, cols),
        out_shape=jax.ShapeDtypeStruct(ut.shape, BF16),
        scratch_shapes=[pltpu.VMEM((r, r), BF16),
                        pltpu.VMEM((V7X_LANES // p, p, 2 * r - V7X_LANES), F32)],
        compiler_params=_params(1),
        name="s5_chunked",
    )(ut, par, bt_re, bt_im, both(c_re), both(c_im))


def _na_body(q_ref, k_ref, v_ref, qg_ref, kg_ref, tab_ref, mask_ref, o_ref,
             qq_ref, kw_ref, vw_ref, sa_ref, sb_ref, *, rows):
    w = GRID_W
    pair = 2 * NA_HEAD_DIM
    head0 = lax.broadcasted_iota(jnp.int32, (1, pair), 1) < NA_HEAD_DIM
    head_mean = jnp.where(lax.broadcasted_iota(jnp.int32, (pair, pair), 0) // NA_HEAD_DIM
                          == lax.broadcasted_iota(jnp.int32, (pair, pair), 1) // NA_HEAD_DIM,
                          1.0 / NA_HEAD_DIM, 0.0).astype(BF16)

    def head_norm(t, gain):
        sq = t * t
        hi = sq.astype(BF16)
        lo = (sq - hi.astype(F32)).astype(BF16)
        ms = _dot(hi, head_mean) + _dot(lo, head_mean)
        return t * lax.rsqrt(ms + NORM_EPS) * gain

    qn = head_norm(q_ref[0].astype(F32), qg_ref[...])
    qh = (jnp.where(head0, qn, 0.0), jnp.where(head0, 0.0, qn))
    kn = head_norm(k_ref[0].astype(F32), kg_ref[...]).astype(BF16)

    cb = NA_WIN_COLS
    n_blocks = w // cb
    lanes = NA_WIN_ROWS * cb
    live = _na_live_tiles()
    vv = v_ref[0]
    for row in range(rows):
        for t in range(n_blocks):
            src = slice(row * w + t * cb, row * w + (t + 1) * cb)
            kw_ref[t, row * cb:(row + 1) * cb, :] = kn[src]
            vw_ref[t, row * cb:(row + 1) * cb, :] = vv[src]
        stacked = jnp.concatenate([t[row * w:(row + 1) * w] for t in qh], axis=0)
        qq_ref[2 * row * w:2 * (row + 1) * w, :] = stacked.astype(BF16)

    n_steps = rows // NA_ROWS_PER_STEP
    sub = V7X_SUBLANES

    def offsets(step, j):
        row = step * NA_ROWS_PER_STEP + j
        start = jnp.clip(row - NA_WIN_ROWS // 2, 0, rows - NA_WIN_ROWS)
        return (row, pl.multiple_of(start * cb, cb), start - row + (NA_WIN_ROWS - 1))

    def window(ref, k_off):
        return jnp.concatenate([ref[t, pl.ds(k_off, lanes), :] for t in range(n_blocks)], axis=0)

    def score_row(step, j, stage_ref):
        row, k_off, _ = offsets(step, j)
        q_off = pl.multiple_of(row * (2 * w), 2 * w)
        stage_ref[j] = _dot_nt(qq_ref[pl.ds(q_off, 2 * w), :], window(kw_ref, k_off))

    def value_row(step, j, stage_ref):
        row, k_off, bias_row = offsets(step, j)
        p_rows, l_rows = [], []
        for h in range(2):
            for a in range(w // sub):
                qs = slice(a * sub, (a + 1) * sub)
                tiles = {}
                for t in live[a]:
                    ks = slice(t * lanes, (t + 1) * lanes)
                    s = stage_ref[j, h * w + a * sub:h * w + (a + 1) * sub, ks]
                    s = s + tab_ref[0, h, bias_row, qs, ks]
                    tiles[t] = jnp.where(mask_ref[qs, ks] != 0, s, MASK_VALUE)
                m = functools.reduce(jnp.maximum, tiles.values())
                m = jnp.maximum(jnp.max(m, axis=-1, keepdims=True), MASK_VALUE)
                tiles = {t: jnp.exp(s - m) for t, s in tiles.items()}
                masked = jnp.exp(MASK_VALUE - m)
                n_masked = (n_blocks - len(tiles)) * lanes
                l_rows.append(jnp.sum(functools.reduce(jnp.add, tiles.values()), axis=-1,
                                      keepdims=True) + n_masked * masked)
                fill = jnp.broadcast_to(masked, (sub, lanes))
                p_rows.append(jnp.concatenate([tiles.get(t, fill) for t in range(n_blocks)], axis=1))
        p = jnp.concatenate(p_rows, axis=0).astype(BF16)
        o = _dot(p, window(vw_ref, k_off)) / jnp.concatenate(l_rows, axis=0)
        q_off = pl.multiple_of(row * w, w)
        o_ref[0, pl.ds(q_off, w), :] = jnp.where(head0, o[:w], o[w:]).astype(o_ref.dtype)

    for j in range(NA_ROWS_PER_STEP):
        score_row(0, j, sa_ref)

    def step_pair(i, carry):
        even = 2 * i
        for j in range(NA_ROWS_PER_STEP):
            score_row(even + 1, j, sb_ref)
            value_row(even, j, sa_ref)
        nxt = jnp.minimum(even + 2, n_steps - 1)
        for j in range(NA_ROWS_PER_STEP):
            score_row(nxt, j, sa_ref)
            value_row(even + 1, j, sb_ref)
        return carry

    lax.fori_loop(0, n_steps // 2, step_pair, 0)


def _na_col_start():
    cols = np.arange(GRID_W)
    return np.clip(cols - NA_WIN_COLS // 2, 0, GRID_W - NA_WIN_COLS)


def _na_col_valid():
    cols = np.arange(GRID_W)
    cs = _na_col_start()[:, None]
    return (cols[None, :] >= cs) & (cols[None, :] < cs + NA_WIN_COLS)


def _na_live_tiles():
    valid = _na_col_valid().reshape(GRID_W // V7X_SUBLANES, V7X_SUBLANES,
                                    GRID_W // NA_WIN_COLS, NA_WIN_COLS)
    return [list(np.nonzero(tile.any(axis=(0, 2)))[0]) for tile in valid]


def _na_bias_tables(rpb):
    w, cb = GRID_W, NA_WIN_COLS
    cols = np.arange(w)
    rel = np.clip(cols[None, :] - cols[:, None], 1 - NA_WIN_COLS, NA_WIN_COLS - 1)
    by_col = rpb[:, :, rel + (NA_WIN_COLS - 1)]
    idx = np.arange(NA_WIN_ROWS)[:, None] + np.arange(NA_WIN_ROWS)[None, :]
    t = by_col[:, idx]
    t = t.reshape(NA_HEADS, NA_WIN_ROWS, NA_WIN_ROWS, w, w // cb, cb)
    t = t.transpose(0, 1, 3, 4, 2, 5)
    return t.reshape(NA_HEADS // 2, 2, NA_WIN_ROWS, w, NA_WIN_ROWS * w)


def _na_col_mask():
    w, cb = GRID_W, NA_WIN_COLS
    m = _na_col_valid().reshape(w, w // cb, 1, cb)
    m = np.broadcast_to(m, (w, w // cb, NA_WIN_ROWS, cb))
    return jnp.asarray(m.reshape(w, NA_WIN_ROWS * w).astype(np.int32))


def _neighbourhood_attention(qkv3, q_gain, k_gain, rpb):
    bsz, seq, d3 = qkv3.shape
    d = d3 // 3
    rows = seq // GRID_W
    assert rows >= NA_WIN_ROWS and rows % (2 * NA_ROWS_PER_STEP) == 0 and d == NA_HEADS * NA_HEAD_DIM
    pair = 2 * NA_HEAD_DIM
    n_pairs = d // pair
    keys = NA_WIN_ROWS * GRID_W
    assert NA_WIN_ROWS * NA_WIN_COLS == V7X_LANES and GRID_W % NA_WIN_COLS == 0
    blk = lambda off: pl.BlockSpec((1, seq, pair), lambda hp, b: (b, 0, off + hp))
    gains = lambda gvec: jnp.tile(gvec, 2).reshape(1, pair)
    stage = pltpu.VMEM((NA_ROWS_PER_STEP, 2 * GRID_W, keys), F32)
    windows = pltpu.VMEM((GRID_W // NA_WIN_COLS, rows * NA_WIN_COLS, pair), BF16)
    queries = pltpu.VMEM((2 * seq, pair), BF16)
    return pl.pallas_call(
        functools.partial(_na_body, rows=rows),
        grid=(n_pairs, bsz),
        in_specs=[blk(0), blk(n_pairs), blk(2 * n_pairs), _resident((1, pair)), _resident((1, pair)),
                  pl.BlockSpec((1, 2, NA_WIN_ROWS, GRID_W, keys), lambda hp, b: (hp, 0, 0, 0, 0)),
                  _resident((GRID_W, keys))],
        out_specs=pl.BlockSpec((1, seq, pair), lambda hp, b: (b, 0, hp)),
        out_shape=jax.ShapeDtypeStruct((bsz, seq, d), BF16),
        scratch_shapes=[queries] + [windows] * 2 + [stage] * 2,
        compiler_params=_params(2),
        name="neighbourhood_attention",
    )(qkv3, qkv3, qkv3, gains(q_gain) * (NA_HEAD_DIM ** -0.5), gains(k_gain),
      _na_bias_tables(rpb), _na_col_mask())


def kernel(x, mix_norm, ffn_norm, ffn_w_gate, ffn_w_up, ffn_w_down, ab_w_in, conv_w, conv_b, conv_ln_g, conv_ln_b, ssm_lambda_re, ssm_lambda_im, ssm_log_step, ssm_b_re, ssm_b_im, ssm_c_re, ssm_c_im, ssm_d, ssm_glu_w, ssm_glu_b, ab_w_out, na_w_qkv, na_q_norm, na_k_norm, na_rpb, na_w_out):
    bsz, seq, d = x.shape
    m = bsz * seq
    chunk = S5_CHUNK
    assert m % ROW_TILE == 0 and seq % chunk == 0
    n_chunks = seq // chunk
    rows_c = bsz * n_chunks
    depth = mix_norm.shape[0]
    bf = lambda w: w.astype(BF16)
    xs = x.reshape(m, d)
    for layer in range(depth):
        i = layer // 2
        ffn = (ffn_norm[layer], bf(ffn_w_gate[layer]), bf(ffn_w_up[layer]), bf(ffn_w_down[layer]))
        if layer % 2 == 0:
            xc = xs.reshape(rows_c, chunk, d)
            z, ut = _ssm_in_proj(xc, mix_norm[layer], bf(ab_w_in[i]))
            ya = _conv_module(z.reshape(bsz, seq, -1), conv_w[i], conv_b[i], conv_ln_g[i],
                              conv_ln_b[i])
            yt = _s5_scan(ut, ssm_lambda_re[i], ssm_lambda_im[i], ssm_log_step[i], ssm_b_re[i],
                          ssm_b_im[i], ssm_c_re[i], ssm_c_im[i], n_chunks)
            ys = _s5_unlayout(yt)
            xs = _ssm_ffn(xs, ya.reshape(m, -1), ys.reshape(m, -1), z.reshape(m, -1), ssm_d[i],
                          bf(ssm_glu_w[i]), ssm_glu_b[i], bf(ab_w_out[i]), *ffn)
        else:
            qkv = _norm_matmul(xs, mix_norm[layer], bf(na_w_qkv[i]))
            attn = _neighbourhood_attention(qkv.reshape(bsz, seq, -1), na_q_norm[i], na_k_norm[i],
                                            na_rpb[i])
            xs = _attn_ffn(xs, attn.reshape(m, d), bf(na_w_out[i]), *ffn)
    return xs.reshape(bsz, seq, d)
```

```python
import functools

import numpy as np
import jax
import jax.numpy as jnp
from jax import lax
from jax.experimental import pallas as pl
from jax.experimental.pallas import tpu as pltpu

F32 = jnp.float32
BF16 = jnp.bfloat16

GRID_W = 64
CONV_DIM = 512
CONV_WIDTH = 31
SSM_DIM = 512
SSM_GROUP = 16
SSM_GROUPS = SSM_DIM // SSM_GROUP
SSM_STATE = 64
NA_HEADS = 16
NA_HEAD_DIM = 64
NA_WIN_ROWS = 8
NA_WIN_COLS = 16
NORM_EPS = 1e-6
LN_EPS = 1e-5
MASK_VALUE = -1e30

S5_CHUNK = 64
S5_STEP_BLOCK = 8
S5_CHUNK_BLOCK = 128
V7X_LANES = 128
V7X_SUBLANES = 8
V7X_VMEM_LIMIT = 56 * 1024 * 1024
ROW_TILE = 512
CONV_TILE = 256
CONV_PAD = 16
NA_ROWS_PER_STEP = 4


def _params(n_axes):
    return pltpu.CompilerParams(dimension_semantics=("arbitrary",) * n_axes,
                                vmem_limit_bytes=V7X_VMEM_LIMIT)


def _resident(shape):
    return pl.BlockSpec(shape, lambda *_: (0,) * len(shape), pipeline_mode=pl.Buffered(1))


def _rms_norm(x, gain):
    ms = jnp.mean(x * x, axis=-1, keepdims=True)
    return x * lax.rsqrt(ms + NORM_EPS) * gain


def _dot(a, b):
    return jnp.dot(a, b, preferred_element_type=F32)


def _dot_nt(a, b):
    return lax.dot_general(a, b, (((1,), (1,)), ((), ())), preferred_element_type=F32)


def _norm_matmul_body(x_ref, g_ref, w_ref, o_ref):
    h = _rms_norm(x_ref[...], g_ref[...]).astype(BF16)
    o_ref[...] = _dot(h, w_ref[...]).astype(o_ref.dtype)


def _norm_matmul(x, gain, w):
    m, d = x.shape
    n = w.shape[1]
    return pl.pallas_call(
        _norm_matmul_body,
        grid=(m // ROW_TILE,),
        in_specs=[pl.BlockSpec((ROW_TILE, d), lambda i: (i, 0)),
                  _resident((1, d)), _resident((d, n))],
        out_specs=pl.BlockSpec((ROW_TILE, n), lambda i: (i, 0)),
        out_shape=jax.ShapeDtypeStruct((m, n), BF16),
        compiler_params=_params(1),
        name="norm_matmul",
    )(x, gain.reshape(1, d), w)


def _ffn_tail(x1, fg_ref, wg_ref, wu_ref, wd_ref):
    h = _rms_norm(x1, fg_ref[...]).astype(BF16)
    g = _dot(h, wg_ref[...])
    u = _dot(h, wu_ref[...])
    a = (g * jax.nn.sigmoid(g) * u).astype(BF16)
    return x1 + _dot(a, wd_ref[...])


def _attn_ffn_body(x_ref, a_ref, wo_ref, fg_ref, wg_ref, wu_ref, wd_ref, o_ref):
    x1 = x_ref[...] + _dot(a_ref[...], wo_ref[...])
    o_ref[...] = _ffn_tail(x1, fg_ref, wg_ref, wu_ref, wd_ref)


def _ssm_ffn_body(x_ref, ya_ref, ys_ref, u_ref, d_ref, gw_ref, gb_ref, wo_ref,
                  fg_ref, wg_ref, wu_ref, wd_ref, o_ref):
    y = ys_ref[...] + d_ref[...] * u_ref[...]
    y = jax.nn.gelu(y, approximate=True)
    gate = jax.nn.sigmoid(_dot(y.astype(BF16), gw_ref[...]) + gb_ref[...])
    yb = (y * gate).astype(BF16)
    x1 = (x_ref[...] + _dot(ya_ref[...], wo_ref[:CONV_DIM, :]) + _dot(yb, wo_ref[CONV_DIM:, :]))
    o_ref[...] = _ffn_tail(x1, fg_ref, wg_ref, wu_ref, wd_ref)


def _row_spec(width, col_block=0):
    return pl.BlockSpec((ROW_TILE, width), lambda i: (i, col_block))


def _attn_ffn(x, attn, w_out, ffn_gain, w_gate, w_up, w_down):
    m, d = x.shape
    f = w_gate.shape[1]
    return pl.pallas_call(
        _attn_ffn_body,
        grid=(m // ROW_TILE,),
        in_specs=[_row_spec(d), _row_spec(d), _resident((d, d)), _resident((1, d)),
                  _resident((d, f)), _resident((d, f)), _resident((f, d))],
        out_specs=_row_spec(d),
        out_shape=jax.ShapeDtypeStruct((m, d), F32),
        compiler_params=_params(1),
        name="attn_out_ffn",
    )(x, attn, w_out, ffn_gain.reshape(1, d), w_gate, w_up, w_down)


def _ssm_ffn(x, ya, ys, z, d_skip, glu_w, glu_b, w_out, ffn_gain, w_gate, w_up, w_down):
    m, d = x.shape
    f = w_gate.shape[1]
    c = SSM_DIM
    return pl.pallas_call(
        _ssm_ffn_body,
        grid=(m // ROW_TILE,),
        in_specs=[_row_spec(d), _row_spec(CONV_DIM), _row_spec(c),
                  _row_spec(c, col_block=2 * CONV_DIM // c),
                  _resident((1, c)), _resident((c, c)), _resident((1, c)),
                  _resident((CONV_DIM + c, d)), _resident((1, d)),
                  _resident((d, f)), _resident((d, f)), _resident((f, d))],
        out_specs=_row_spec(d),
        out_shape=jax.ShapeDtypeStruct((m, d), F32),
        compiler_params=_params(1),
        name="ssm_out_ffn",
    )(x, ya, ys, z, d_skip.reshape(1, c), glu_w, glu_b.reshape(1, c), w_out,
      ffn_gain.reshape(1, d), w_gate, w_up, w_down)


def _token_block_spec(width):
    return pl.BlockSpec((S5_CHUNK_BLOCK, S5_STEP_BLOCK, width), lambda a, cb: (cb, a, 0))


def _slab_block_spec():
    return pl.BlockSpec((SSM_GROUPS, S5_STEP_BLOCK, SSM_GROUP, S5_CHUNK_BLOCK),
                        lambda a, cb: (0, a, 0, cb))


def _ssm_in_proj_body(x_ref, g_ref, w_ref, z_ref, ut_ref, u_scr):
    cb, st, d = x_ref.shape
    h = _rms_norm(x_ref[...].reshape(cb * st, d), g_ref[...]).astype(BF16)
    z = _dot(h, w_ref[...])
    z_ref[...] = z.reshape(cb, st, -1)
    lanes = V7X_LANES
    groups = lanes // SSM_GROUP
    for l in range(SSM_DIM // lanes):
        u_scr[l] = z[:, 2 * CONV_DIM + l * lanes:2 * CONV_DIM + (l + 1) * lanes]
        for t in range(st):
            ut = u_scr[l, pl.ds(t, cb, stride=st), :].T
            ut_ref[l * groups:(l + 1) * groups, t] = (
                ut.reshape(groups, SSM_GROUP, cb).astype(ut_ref.dtype))


def _ssm_in_proj(xc, gain, w):
    rows, chunk, d = xc.shape
    n = w.shape[1]
    assert rows % S5_CHUNK_BLOCK == 0 and chunk % S5_STEP_BLOCK == 0
    return pl.pallas_call(
        _ssm_in_proj_body,
        grid=(chunk // S5_STEP_BLOCK, rows // S5_CHUNK_BLOCK),
        in_specs=[_token_block_spec(d), _resident((1, d)), _resident((d, n))],
        out_specs=[_token_block_spec(n), _slab_block_spec()],
        out_shape=[jax.ShapeDtypeStruct((rows, chunk, n), F32),
                   jax.ShapeDtypeStruct((SSM_GROUPS, chunk, SSM_GROUP, rows), BF16)],
        scratch_shapes=[pltpu.VMEM((SSM_DIM // V7X_LANES, S5_CHUNK_BLOCK * S5_STEP_BLOCK, V7X_LANES),
                                   F32)],
        compiler_params=_params(2),
        name="ssm_in_proj",
    )(xc, gain.reshape(1, d), w)


def _s5_unlayout_body(yt_ref, ys_ref, y_scr):
    g, st, p, cb = yt_ref.shape
    lanes = V7X_LANES
    groups = lanes // p
    for l in range(g * p // lanes):
        for t in range(st):
            y = yt_ref[l * groups:(l + 1) * groups, t].astype(F32).reshape(lanes, cb).T
            y_scr[l, pl.ds(t, cb, stride=st), :] = y
        ys_ref[:, :, l * lanes:(l + 1) * lanes] = y_scr[l].reshape(cb, st, lanes)


def _s5_unlayout(yt):
    g, chunk, p, rows = yt.shape
    return pl.pallas_call(
        _s5_unlayout_body,
        grid=(chunk // S5_STEP_BLOCK, rows // S5_CHUNK_BLOCK),
        in_specs=[_slab_block_spec()],
        out_specs=_token_block_spec(g * p),
        out_shape=jax.ShapeDtypeStruct((rows, chunk, g * p), F32),
        scratch_shapes=[pltpu.VMEM((g * p // V7X_LANES, S5_CHUNK_BLOCK * S5_STEP_BLOCK, V7X_LANES),
                                   F32)],
        compiler_params=_params(2),
        name="s5_unlayout",
    )(yt)


def _conv_body(za_ref, zg_ref, w_ref, cb_ref, lg_ref, lb_ref, o_ref, apad_ref, *, seq):
    c = CONV_DIM
    apad_ref[0:CONV_PAD, :] = jnp.zeros((CONV_PAD, c), F32)
    apad_ref[CONV_PAD + seq:, :] = jnp.zeros((CONV_PAD, c), F32)

    glu_tile = 256

    def glu_step(i, carry):
        r0 = pl.multiple_of(i * glu_tile, glu_tile)
        za = za_ref[0, pl.ds(r0, glu_tile), :].astype(F32)
        zg = zg_ref[0, pl.ds(r0, glu_tile), :].astype(F32)
        apad_ref[pl.ds(CONV_PAD + r0, glu_tile), :] = za * jax.nn.sigmoid(zg)
        return carry

    lax.fori_loop(0, seq // glu_tile, glu_step, 0)

    first = CONV_PAD - CONV_WIDTH // 2
    sub = V7X_SUBLANES

    def conv_step(i, carry):
        r0 = pl.multiple_of(i * CONV_TILE, CONV_TILE)
        win = apad_ref[pl.ds(r0, CONV_TILE + 2 * CONV_PAD), :]
        acc = cb_ref[...]
        for r in range(sub):
            part = None
            for k in range(r, CONV_WIDTH, sub):
                term = w_ref[k:k + 1, :] * win[k - r:k - r + CONV_TILE + sub]
                part = term if part is None else part + term
            acc = acc + part[first + r:first + r + CONV_TILE]
        mu = jnp.mean(acc, axis=-1, keepdims=True)
        xc = acc - mu
        var = jnp.mean(xc * xc, axis=-1, keepdims=True)
        y = xc * lax.rsqrt(var + LN_EPS) * lg_ref[...] + lb_ref[...]
        o_ref[0, pl.ds(r0, CONV_TILE), :] = (y * jax.nn.sigmoid(y)).astype(o_ref.dtype)
        return carry

    lax.fori_loop(0, seq // CONV_TILE, conv_step, 0)


def _conv_module(z3, conv_w, conv_b, ln_g, ln_b):
    bsz, seq, _ = z3.shape
    c = CONV_DIM
    return pl.pallas_call(
        functools.partial(_conv_body, seq=seq),
        grid=(bsz,),
        in_specs=[pl.BlockSpec((1, seq, c), lambda b: (b, 0, 0)),
                  pl.BlockSpec((1, seq, c), lambda b: (b, 0, 1)),
                  _resident((CONV_WIDTH, c)), _resident((1, c)), _resident((1, c)),
                  _resident((1, c))],
        out_specs=pl.BlockSpec((1, seq, c), lambda b: (b, 0, 0)),
        out_shape=jax.ShapeDtypeStruct((bsz, seq, c), BF16),
        scratch_shapes=[pltpu.VMEM((seq + 2 * CONV_PAD, c), F32)],
        compiler_params=_params(1),
        name="conformer_conv",
    )(z3, z3, conv_w, conv_b.reshape(1, c), ln_g.reshape(1, c), ln_b.reshape(1, c))


def _s5_body(xt_ref, par_ref, btr_ref, bti_ref, cr_ref, ci_ref, yt_ref, t_ref, w8_ref,
             *, chunk, n_chunks):
    n = SSM_STATE
    n2 = 2 * n
    p = SSM_GROUP
    r = p * chunk
    fwd = lax.broadcasted_iota(jnp.int32, (1, n2), 1) < n
    step = lax.broadcasted_iota(jnp.int32, (chunk, 1), 0).astype(F32)

    def split(a, terms):
        out = []
        for _ in range(terms):
            hi = a.astype(BF16)
            out.append(hi)
            a = a - hi.astype(F32)
        return out

    def expand(a):
        return jnp.concatenate(
            [jnp.broadcast_to(a[j:j + 1, :], (p, a.shape[1])) for j in range(chunk)], axis=0)

    def tile_rows(a):
        return jnp.concatenate([a] * chunk, axis=0)

    lr = par_ref[0, 0:1, :]
    li = par_ref[0, 1:2, :]
    dt = jnp.exp(par_ref[0, 2:3, :])
    lr_dt = lr * dt
    li_dt = li * dt

    def cpow(e):
        mag = jnp.exp(lr_dt * e)
        ang = li_dt * e
        return mag * jnp.cos(ang), mag * jnp.sin(ang)

    ar, ai = cpow(1.0)
    den = lr * lr + li * li
    pr = ar - 1.0
    coef_re = (pr * lr + ai * li) / den
    coef_im = (ai * lr - pr * li) / den
    btr = btr_ref[0]
    bti = bti_ref[0]
    bbr = tile_rows(coef_re * btr - coef_im * bti)
    bbi = tile_rows(coef_re * bti + coef_im * btr)
    ctr = tile_rows(cr_ref[0])
    cti = tile_rows(ci_ref[0])
    e_m = jnp.where(fwd, (chunk - 1.0) - step, step)
    e_q = jnp.where(fwd, step + 1.0, chunk - step)
    powers = expand(jnp.concatenate(cpow(e_m) + cpow(e_q), axis=1))
    pmr, pmi, pqr, pqi = (powers[:, i * n2:(i + 1) * n2] for i in range(4))
    mr = bbr * pmr - bbi * pmi
    mi = bbr * pmi + bbi * pmr
    qr = ctr * pqr - cti * pqi
    qi = ctr * pqi + cti * pqr

    def strip_rows(v):
        zero = jnp.zeros_like(v)
        f = jnp.where(fwd, v, zero)
        b = jnp.where(fwd, zero, v)
        return (jnp.concatenate([f, zero], axis=0)
                + jnp.concatenate([zero[:r - p], b, zero[:p]], axis=0))

    def dot_nt_split(a, b_terms):
        a_hi, a_lo = split(a, 2)
        b_hi, b_lo = b_terms
        return _dot_nt(a_hi, b_hi) + _dot_nt(a_hi, b_lo) + _dot_nt(a_lo, b_hi)

    mr_terms = split(mr, 2)
    mi_terms = split(mi, 2)
    wc = (dot_nt_split(cr_ref[0], [strip_rows(v) for v in mr_terms])
          - dot_nt_split(ci_ref[0], [strip_rows(v) for v in mi_terms]))

    width = 2 * r - V7X_LANES
    for s in range(V7X_LANES // p):
        w8_ref[s] = wc[:, p * s:p * s + width]
    for t in range(chunk):
        a, s = divmod(p * (chunk - 1 - t), V7X_LANES)
        t_ref[p * t:p * (t + 1), :] = w8_ref[s // p, :, V7X_LANES * a:V7X_LANES * a + r].astype(BF16)

    cols = xt_ref.shape[-1]
    xt = xt_ref[0].reshape(r, cols)
    y = _dot(t_ref[...], xt)

    e = _dot(jnp.concatenate([mr, mi], axis=1).T.astype(BF16), xt)
    lvl = lax.broadcasted_iota(jnp.int32, (V7X_SUBLANES, 1), 0)
    lre, lim = cpow((chunk * jnp.left_shift(1, lvl)).astype(F32))
    pcol = jnp.concatenate([lre, lim, jnp.zeros((n2 - 2 * V7X_SUBLANES, n2), F32)], axis=0).T
    jl = lax.broadcasted_iota(jnp.int32, (1, cols), 1) % n_chunks

    prev = {}
    for d in range(2):
        rows_d = slice(d * n, (d + 1) * n)
        sr, si = e[rows_d], e[n2 + d * n:n2 + (d + 1) * n]
        shifted = lambda v, s: pltpu.roll(v, s if d == 0 else cols - s, axis=1)
        k, s = 0, 1
        while s < n_chunks:
            keep = (jl >= s) if d == 0 else (jl < n_chunks - s)
            a_r = pcol[rows_d, k:k + 1]
            a_i = pcol[rows_d, V7X_SUBLANES + k:V7X_SUBLANES + k + 1]
            rr, ri = shifted(sr, s), shifted(si, s)
            sr, si = (sr + jnp.where(keep, a_r * rr - a_i * ri, 0.0),
                      si + jnp.where(keep, a_r * ri + a_i * rr, 0.0))
            k, s = k + 1, 2 * s
        keep = (jl >= 1) if d == 0 else (jl < n_chunks - 1)
        prev[d] = (jnp.where(keep, shifted(sr, 1), 0.0), jnp.where(keep, shifted(si, 1), 0.0))

    s_prev = jnp.concatenate([prev[0][0], prev[1][0], prev[0][1], prev[1][1]], axis=0)
    q_cat = jnp.concatenate([qr, -qi], axis=1)
    y = y + _dot(q_cat.astype(BF16), s_prev.astype(BF16))
    yt_ref[0] = y.reshape(chunk, p, cols).astype(yt_ref.dtype)


def _s5_scan(ut, lam_re, lam_im, log_step, b_re, b_im, c_re, c_im, n_chunks):
    g, chunk, p, cols = ut.shape
    n2 = 2 * SSM_STATE
    assert cols % V7X_LANES == 0 and n_chunks <= 2 ** V7X_SUBLANES
    r = p * chunk
    both = lambda a: a.transpose(1, 2, 0, 3).reshape(g, -1, n2)
    par = both(jnp.stack([lam_re, lam_im, jnp.broadcast_to(log_step[..., None], lam_re.shape)],
                         axis=2))
    bt_re = both(b_re.transpose(0, 1, 3, 2))
    bt_im = both(b_im.transpose(0, 1, 3, 2))
    grp = lambda *tail: pl.BlockSpec((1,) + tail, lambda i: (i,) + (0,) * len(tail))
    return pl.pallas_call(
        functools.partial(_s5_body, chunk=chunk, n_chunks=n_chunks),
        grid=(g,),
        in_specs=[grp(chunk, p, cols), grp(3, n2), grp(p, n2), grp(p, n2), grp(p, n2), grp(p, n2)],
        out_specs=grp(chunk, p, cols),
        out_shape=jax.ShapeDtypeStruct(ut.shape, BF16),
        scratch_shapes=[pltpu.VMEM((r, r), BF16),
                        pltpu.VMEM((V7X_LANES // p, p, 2 * r - V7X_LANES), F32)],
        compiler_params=_params(1),
        name="s5_chunked",
    )(ut, par, bt_re, bt_im, both(c_re), both(c_im))


def _na_body(q_ref, k_ref, v_ref, qg_ref, kg_ref, tab_ref, mask_ref, o_ref,
             qq_ref, kw_ref, vw_ref, sa_ref, sb_ref, *, rows):
    w = GRID_W
    pair = 2 * NA_HEAD_DIM
    head0 = lax.broadcasted_iota(jnp.int32, (1, pair), 1) < NA_HEAD_DIM
    head_mean = jnp.where(lax.broadcasted_iota(jnp.int32, (pair, pair), 0) // NA_HEAD_DIM
                          == lax.broadcasted_iota(jnp.int32, (pair, pair), 1) // NA_HEAD_DIM,
                          1.0 / NA_HEAD_DIM, 0.0).astype(BF16)

    def head_norm(t, gain):
        sq = t * t
        hi = sq.astype(BF16)
        lo = (sq - hi.astype(F32)).astype(BF16)
        ms = _dot(hi, head_mean) + _dot(lo, head_mean)
        return t * lax.rsqrt(ms + NORM_EPS) * gain

    qn = head_norm(q_ref[0].astype(F32), qg_ref[...])
    qh = (jnp.where(head0, qn, 0.0), jnp.where(head0, 0.0, qn))
    kn = head_norm(k_ref[0].astype(F32), kg_ref[...]).astype(BF16)

    cb = NA_WIN_COLS
    sub = V7X_SUBLANES
    n_blocks = w // cb
    lanes = NA_WIN_ROWS * cb
    live = _na_live_tiles()
    vv = v_ref[0]
    for row in range(rows):
        for t in range(n_blocks):
            src = slice(row * w + t * cb, row * w + (t + 1) * cb)
            kw_ref[t, row * cb:(row + 1) * cb, :] = kn[src]
            vw_ref[t, row * cb:(row + 1) * cb, :] = vv[src]
        stacked = jnp.concatenate([t[row * w:(row + 1) * w] for t in qh], axis=0)
        qq_ref[2 * row * w:2 * (row + 1) * w, :] = stacked.astype(BF16)

    n_steps = rows // NA_ROWS_PER_STEP

    def offsets(step, j):
        row = step * NA_ROWS_PER_STEP + j
        start = jnp.clip(row - NA_WIN_ROWS // 2, 0, rows - NA_WIN_ROWS)
        return (row, pl.multiple_of(start * cb, cb), start - row + (NA_WIN_ROWS - 1))

    def window(ref, k_off):
        return jnp.concatenate([ref[t, pl.ds(k_off, lanes), :] for t in range(n_blocks)], axis=0)

    def score_row(step, j, stage_ref):
        row, k_off, _ = offsets(step, j)
        q_off = pl.multiple_of(row * (2 * w), 2 * w)
        stage_ref[j] = _dot_nt(qq_ref[pl.ds(q_off, 2 * w), :], window(kw_ref, k_off))

    def value_row(step, j, stage_ref):
        row, k_off, bias_row = offsets(step, j)
        p_rows, l_rows = [], []
        for h in range(2):
            for a in range(w // sub):
                qs = slice(a * sub, (a + 1) * sub)
                tiles = {}
                for t in live[a]:
                    ks = slice(t * lanes, (t + 1) * lanes)
                    s = stage_ref[j, h * w + a * sub:h * w + (a + 1) * sub, ks]
                    s = s + tab_ref[0, h, bias_row, qs, ks]
                    tiles[t] = jnp.where(mask_ref[qs, ks] != 0, s, MASK_VALUE)
                m = functools.reduce(jnp.maximum, tiles.values())
                m = jnp.maximum(jnp.max(m, axis=-1, keepdims=True), MASK_VALUE)
                tiles = {t: jnp.exp(s - m) for t, s in tiles.items()}
                masked = jnp.exp(MASK_VALUE - m)
                n_masked = (n_blocks - len(tiles)) * lanes
                l_rows.append(jnp.sum(functools.reduce(jnp.add, tiles.values()), axis=-1,
                                      keepdims=True) + n_masked * masked)
                fill = jnp.broadcast_to(masked, (sub, lanes))
                p_rows.append(jnp.concatenate([tiles.get(t, fill) for t in range(n_blocks)], axis=1))
        p = jnp.concatenate(p_rows, axis=0).astype(BF16)
        o = _dot(p, window(vw_ref, k_off)) / jnp.concatenate(l_rows, axis=0)
        q_off = pl.multiple_of(row * w, w)
        o_ref[0, pl.ds(q_off, w), :] = jnp.where(head0, o[:w], o[w:]).astype(o_ref.dtype)

    for j in range(NA_ROWS_PER_STEP):
        score_row(0, j, sa_ref)

    def step_pair(i, carry):
        even = 2 * i
        for j in range(NA_ROWS_PER_STEP):
            score_row(even + 1, j, sb_ref)
            value_row(even, j, sa_ref)
        nxt = jnp.minimum(even + 2, n_steps - 1)
        for j in range(NA_ROWS_PER_STEP):
            score_row(nxt, j, sa_ref)
            value_row(even + 1, j, sb_ref)
        return carry

    lax.fori_loop(0, n_steps // 2, step_pair, 0)


def _na_col_start():
    cols = np.arange(GRID_W)
    return np.clip(cols - NA_WIN_COLS // 2, 0, GRID_W - NA_WIN_COLS)


def _na_col_valid():
    cols = np.arange(GRID_W)
    cs = _na_col_start()[:, None]
    return (cols[None, :] >= cs) & (cols[None, :] < cs + NA_WIN_COLS)


def _na_live_tiles():
    valid = _na_col_valid().reshape(GRID_W // V7X_SUBLANES, V7X_SUBLANES,
                                    GRID_W // NA_WIN_COLS, NA_WIN_COLS)
    return [list(np.nonzero(tile.any(axis=(0, 2)))[0]) for tile in valid]


def _na_bias_tables(rpb):
    w, cb = GRID_W, NA_WIN_COLS
    i, q, t, j, c = np.ix_(np.arange(NA_WIN_ROWS), np.arange(w), np.arange(w // cb),
                           np.arange(NA_WIN_ROWS), np.arange(cb))
    rel = np.clip(t * cb + c - q, 1 - NA_WIN_COLS, NA_WIN_COLS - 1) + (NA_WIN_COLS - 1)
    flat = ((i + j) * rpb.shape[2] + rel).reshape(NA_WIN_ROWS, w, NA_WIN_ROWS * w)
    table = jnp.take(rpb.reshape(rpb.shape[0], -1), jnp.asarray(flat, jnp.int32), axis=1)
    return table.reshape(NA_HEADS // 2, 2, NA_WIN_ROWS, w, NA_WIN_ROWS * w)


def _na_col_mask():
    w, cb = GRID_W, NA_WIN_COLS
    m = _na_col_valid().reshape(w, w // cb, 1, cb)
    m = np.broadcast_to(m, (w, w // cb, NA_WIN_ROWS, cb))
    return jnp.asarray(m.reshape(w, NA_WIN_ROWS * w).astype(np.int32))


def _neighbourhood_attention(qkv3, q_gain, k_gain, rpb):
    bsz, seq, d3 = qkv3.shape
    d = d3 // 3
    rows = seq // GRID_W
    assert rows >= NA_WIN_ROWS and rows % (2 * NA_ROWS_PER_STEP) == 0 and d == NA_HEADS * NA_HEAD_DIM
    pair = 2 * NA_HEAD_DIM
    n_pairs = d // pair
    keys = NA_WIN_ROWS * GRID_W
    assert NA_WIN_ROWS * NA_WIN_COLS == V7X_LANES and GRID_W % NA_WIN_COLS == 0
    blk = lambda off: pl.BlockSpec((1, seq, pair), lambda hp, b: (b, 0, off + hp))
    gains = lambda gvec: jnp.tile(gvec, 2).reshape(1, pair)
    stage = pltpu.VMEM((NA_ROWS_PER_STEP, 2 * GRID_W, keys), F32)
    windows = pltpu.VMEM((GRID_W // NA_WIN_COLS, rows * NA_WIN_COLS, pair), BF16)
    queries = pltpu.VMEM((2 * seq, pair), BF16)
    return pl.pallas_call(
        functools.partial(_na_body, rows=rows),
        grid=(n_pairs, bsz),
        in_specs=[blk(0), blk(n_pairs), blk(2 * n_pairs), _resident((1, pair)), _resident((1, pair)),
                  pl.BlockSpec((1, 2, NA_WIN_ROWS, GRID_W, keys), lambda hp, b: (hp, 0, 0, 0, 0)),
                  _resident((GRID_W, keys))],
        out_specs=pl.BlockSpec((1, seq, pair), lambda hp, b: (b, 0, hp)),
        out_shape=jax.ShapeDtypeStruct((bsz, seq, d), BF16),
        scratch_shapes=[queries] + [windows] * 2 + [stage] * 2,
        compiler_params=_params(2),
        name="neighbourhood_attention",
    )(qkv3, qkv3, qkv3, gains(q_gain) * (NA_HEAD_DIM ** -0.5), gains(k_gain),
      _na_bias_tables(rpb), _na_col_mask())


def kernel(x, mix_norm, ffn_norm, ffn_w_gate, ffn_w_up, ffn_w_down, ab_w_in, conv_w, conv_b, conv_ln_g, conv_ln_b, ssm_lambda_re, ssm_lambda_im, ssm_log_step, ssm_b_re, ssm_b_im, ssm_c_re, ssm_c_im, ssm_d, ssm_glu_w, ssm_glu_b, ab_w_out, na_w_qkv, na_q_norm, na_k_norm, na_rpb, na_w_out):
    bsz, seq, d = x.shape
    m = bsz * seq
    chunk = S5_CHUNK
    assert m % ROW_TILE == 0 and seq % chunk == 0
    n_chunks = seq // chunk
    rows_c = bsz * n_chunks
    depth = mix_norm.shape[0]
    bf = lambda w: w.astype(BF16)
    xs = x.reshape(m, d)
    for layer in range(depth):
        i = layer // 2
        ffn = (ffn_norm[layer], bf(ffn_w_gate[layer]), bf(ffn_w_up[layer]), bf(ffn_w_down[layer]))
        if layer % 2 == 0:
            xc = xs.reshape(rows_c, chunk, d)
            z, ut = _ssm_in_proj(xc, mix_norm[layer], bf(ab_w_in[i]))
            ya = _conv_module(z.reshape(bsz, seq, -1), conv_w[i], conv_b[i], conv_ln_g[i],
                              conv_ln_b[i])
            yt = _s5_scan(ut, ssm_lambda_re[i], ssm_lambda_im[i], ssm_log_step[i], ssm_b_re[i],
                          ssm_b_im[i], ssm_c_re[i], ssm_c_im[i], n_chunks)
            ys = _s5_unlayout(yt)
            xs = _ssm_ffn(xs, ya.reshape(m, -1), ys.reshape(m, -1), z.reshape(m, -1), ssm_d[i],
                          bf(ssm_glu_w[i]), ssm_glu_b[i], bf(ab_w_out[i]), *ffn)
        else:
            qkv = _norm_matmul(xs, mix_norm[layer], bf(na_w_qkv[i]))
            attn = _neighbourhood_attention(qkv.reshape(bsz, seq, -1), na_q_norm[i], na_k_norm[i],
                                            na_rpb[i])
            xs = _attn_ffn(xs, attn.reshape(m, d), bf(na_w_out[i]), *ffn)
    return xs.reshape(bsz, seq, d)
```

```python
import functools

import numpy as np
import jax
import jax.numpy as jnp
from jax import lax
from jax.experimental import pallas as pl
from jax.experimental.pallas import tpu as pltpu

F32 = jnp.float32
BF16 = jnp.bfloat16

GRID_W = 64
CONV_DIM = 512
CONV_WIDTH = 31
SSM_DIM = 512
SSM_GROUP = 16
SSM_GROUPS = SSM_DIM // SSM_GROUP
SSM_STATE = 64
NA_HEADS = 16
NA_HEAD_DIM = 64
NA_WIN_ROWS = 8
NA_WIN_COLS = 16
NORM_EPS = 1e-6
LN_EPS = 1e-5
MASK_VALUE = -1e30

S5_CHUNK = 64
S5_STEP_BLOCK = 8
S5_CHUNK_BLOCK = 128
V7X_LANES = 128
V7X_SUBLANES = 8
V7X_VMEM_LIMIT = 56 * 1024 * 1024
ROW_TILE = 512
CONV_TILE = 256
CONV_PAD = 16
NA_ROWS_PER_STEP = 4


def _params(n_axes):
    return pltpu.CompilerParams(dimension_semantics=("arbitrary",) * n_axes,
                                vmem_limit_bytes=V7X_VMEM_LIMIT)


def _resident(shape):
    return pl.BlockSpec(shape, lambda *_: (0,) * len(shape), pipeline_mode=pl.Buffered(1))


def _rms_norm(x, gain):
    ms = jnp.mean(x * x, axis=-1, keepdims=True)
    return x * lax.rsqrt(ms + NORM_EPS) * gain


def _dot(a, b):
    return jnp.dot(a, b, preferred_element_type=F32)


def _dot_nt(a, b):
    return lax.dot_general(a, b, (((1,), (1,)), ((), ())), preferred_element_type=F32)


def _norm_matmul_body(x_ref, g_ref, w_ref, o_ref):
    h = _rms_norm(x_ref[...], g_ref[...]).astype(BF16)
    o_ref[...] = _dot(h, w_ref[...]).astype(o_ref.dtype)


def _norm_matmul(x, gain, w):
    m, d = x.shape
    n = w.shape[1]
    return pl.pallas_call(
        _norm_matmul_body,
        grid=(m // ROW_TILE,),
        in_specs=[pl.BlockSpec((ROW_TILE, d), lambda i: (i, 0)),
                  _resident((1, d)), _resident((d, n))],
        out_specs=pl.BlockSpec((ROW_TILE, n), lambda i: (i, 0)),
        out_shape=jax.ShapeDtypeStruct((m, n), BF16),
        compiler_params=_params(1),
        name="norm_matmul",
    )(x, gain.reshape(1, d), w)


def _ffn_tail(x1, fg_ref, wg_ref, wu_ref, wd_ref):
    h = _rms_norm(x1, fg_ref[...]).astype(BF16)
    g = _dot(h, wg_ref[...])
    u = _dot(h, wu_ref[...])
    a = (g * jax.nn.sigmoid(g) * u).astype(BF16)
    return x1 + _dot(a, wd_ref[...])


def _attn_ffn_body(x_ref, a_ref, wo_ref, fg_ref, wg_ref, wu_ref, wd_ref, o_ref):
    x1 = x_ref[...] + _dot(a_ref[...], wo_ref[...])
    o_ref[...] = _ffn_tail(x1, fg_ref, wg_ref, wu_ref, wd_ref)


def _ssm_ffn_body(x_ref, ya_ref, ys_ref, u_ref, d_ref, gw_ref, gb_ref, wo_ref,
                  fg_ref, wg_ref, wu_ref, wd_ref, o_ref):
    y = ys_ref[...] + d_ref[...] * u_ref[...]
    y = jax.nn.gelu(y, approximate=True)
    gate = jax.nn.sigmoid(_dot(y.astype(BF16), gw_ref[...]) + gb_ref[...])
    yb = (y * gate).astype(BF16)
    x1 = (x_ref[...] + _dot(ya_ref[...], wo_ref[:CONV_DIM, :]) + _dot(yb, wo_ref[CONV_DIM:, :]))
    o_ref[...] = _ffn_tail(x1, fg_ref, wg_ref, wu_ref, wd_ref)


def _row_spec(width, col_block=0):
    return pl.BlockSpec((ROW_TILE, width), lambda i: (i, col_block))


def _attn_ffn(x, attn, w_out, ffn_gain, w_gate, w_up, w_down):
    m, d = x.shape
    f = w_gate.shape[1]
    return pl.pallas_call(
        _attn_ffn_body,
        grid=(m // ROW_TILE,),
        in_specs=[_row_spec(d), _row_spec(d), _resident((d, d)), _resident((1, d)),
                  _resident((d, f)), _resident((d, f)), _resident((f, d))],
        out_specs=_row_spec(d),
        out_shape=jax.ShapeDtypeStruct((m, d), F32),
        compiler_params=_params(1),
        name="attn_out_ffn",
    )(x, attn, w_out, ffn_gain.reshape(1, d), w_gate, w_up, w_down)


def _ssm_ffn(x, ya, ys, z, d_skip, glu_w, glu_b, w_out, ffn_gain, w_gate, w_up, w_down):
    m, d = x.shape
    f = w_gate.shape[1]
    c = SSM_DIM
    return pl.pallas_call(
        _ssm_ffn_body,
        grid=(m // ROW_TILE,),
        in_specs=[_row_spec(d), _row_spec(CONV_DIM), _row_spec(c),
                  _row_spec(c, col_block=2 * CONV_DIM // c),
                  _resident((1, c)), _resident((c, c)), _resident((1, c)),
                  _resident((CONV_DIM + c, d)), _resident((1, d)),
                  _resident((d, f)), _resident((d, f)), _resident((f, d))],
        out_specs=_row_spec(d),
        out_shape=jax.ShapeDtypeStruct((m, d), F32),
        compiler_params=_params(1),
        name="ssm_out_ffn",
    )(x, ya, ys, z, d_skip.reshape(1, c), glu_w, glu_b.reshape(1, c), w_out,
      ffn_gain.reshape(1, d), w_gate, w_up, w_down)


def _token_block_spec(width):
    return pl.BlockSpec((S5_CHUNK_BLOCK, S5_STEP_BLOCK, width), lambda a, cb: (cb, a, 0))


def _slab_block_spec():
    return pl.BlockSpec((SSM_GROUPS, S5_STEP_BLOCK, SSM_GROUP, S5_CHUNK_BLOCK),
                        lambda a, cb: (0, a, 0, cb))


def _ssm_in_proj_body(x_ref, g_ref, w_ref, z_ref, ut_ref, u_scr):
    cb, st, d = x_ref.shape
    h = _rms_norm(x_ref[...].reshape(cb * st, d), g_ref[...]).astype(BF16)
    z = _dot(h, w_ref[...])
    z_ref[...] = z.reshape(cb, st, -1)
    lanes = V7X_LANES
    groups = lanes // SSM_GROUP
    for l in range(SSM_DIM // lanes):
        u_scr[l] = z[:, 2 * CONV_DIM + l * lanes:2 * CONV_DIM + (l + 1) * lanes]
        for t in range(st):
            ut = u_scr[l, pl.ds(t, cb, stride=st), :].T
            ut_ref[l * groups:(l + 1) * groups, t] = (
                ut.reshape(groups, SSM_GROUP, cb).astype(ut_ref.dtype))


def _ssm_in_proj(xc, gain, w):
    rows, chunk, d = xc.shape
    n = w.shape[1]
    assert rows % S5_CHUNK_BLOCK == 0 and chunk % S5_STEP_BLOCK == 0
    return pl.pallas_call(
        _ssm_in_proj_body,
        grid=(chunk // S5_STEP_BLOCK, rows // S5_CHUNK_BLOCK),
        in_specs=[_token_block_spec(d), _resident((1, d)), _resident((d, n))],
        out_specs=[_token_block_spec(n), _slab_block_spec()],
        out_shape=[jax.ShapeDtypeStruct((rows, chunk, n), F32),
                   jax.ShapeDtypeStruct((SSM_GROUPS, chunk, SSM_GROUP, rows), BF16)],
        scratch_shapes=[pltpu.VMEM((SSM_DIM // V7X_LANES, S5_CHUNK_BLOCK * S5_STEP_BLOCK, V7X_LANES),
                                   F32)],
        compiler_params=_params(2),
        name="ssm_in_proj",
    )(xc, gain.reshape(1, d), w)


def _s5_unlayout_body(yt_ref, ys_ref, y_scr):
    g, st, p, cb = yt_ref.shape
    lanes = V7X_LANES
    groups = lanes // p
    for l in range(g * p // lanes):
        for t in range(st):
            y = yt_ref[l * groups:(l + 1) * groups, t].astype(F32).reshape(lanes, cb).T
            y_scr[l, pl.ds(t, cb, stride=st), :] = y
        ys_ref[:, :, l * lanes:(l + 1) * lanes] = y_scr[l].reshape(cb, st, lanes)


def _s5_unlayout(yt):
    g, chunk, p, rows = yt.shape
    return pl.pallas_call(
        _s5_unlayout_body,
        grid=(chunk // S5_STEP_BLOCK, rows // S5_CHUNK_BLOCK),
        in_specs=[_slab_block_spec()],
        out_specs=_token_block_spec(g * p),
        out_shape=jax.ShapeDtypeStruct((rows, chunk, g * p), F32),
        scratch_shapes=[pltpu.VMEM((g * p // V7X_LANES, S5_CHUNK_BLOCK * S5_STEP_BLOCK, V7X_LANES),
                                   F32)],
        compiler_params=_params(2),
        name="s5_unlayout",
    )(yt)


def _conv_body(za_ref, zg_ref, w_ref, cb_ref, lg_ref, lb_ref, o_ref, apad_ref, *, seq):
    c = CONV_DIM
    apad_ref[0:CONV_PAD, :] = jnp.zeros((CONV_PAD, c), F32)
    apad_ref[CONV_PAD + seq:, :] = jnp.zeros((CONV_PAD, c), F32)

    glu_tile = 256

    def glu_step(i, carry):
        r0 = pl.multiple_of(i * glu_tile, glu_tile)
        za = za_ref[0, pl.ds(r0, glu_tile), :].astype(F32)
        zg = zg_ref[0, pl.ds(r0, glu_tile), :].astype(F32)
        apad_ref[pl.ds(CONV_PAD + r0, glu_tile), :] = za * jax.nn.sigmoid(zg)
        return carry

    lax.fori_loop(0, seq // glu_tile, glu_step, 0)

    first = CONV_PAD - CONV_WIDTH // 2
    sub = V7X_SUBLANES

    def conv_step(i, carry):
        r0 = pl.multiple_of(i * CONV_TILE, CONV_TILE)
        win = apad_ref[pl.ds(r0, CONV_TILE + 2 * CONV_PAD), :]
        acc = cb_ref[...]
        for r in range(sub):
            part = None
            for k in range(r, CONV_WIDTH, sub):
                term = w_ref[k:k + 1, :] * win[k - r:k - r + CONV_TILE + sub]
                part = term if part is None else part + term
            acc = acc + part[first + r:first + r + CONV_TILE]
        mu = jnp.mean(acc, axis=-1, keepdims=True)
        xc = acc - mu
        var = jnp.mean(xc * xc, axis=-1, keepdims=True)
        y = xc * lax.rsqrt(var + LN_EPS) * lg_ref[...] + lb_ref[...]
        o_ref[0, pl.ds(r0, CONV_TILE), :] = (y * jax.nn.sigmoid(y)).astype(o_ref.dtype)
        return carry

    lax.fori_loop(0, seq // CONV_TILE, conv_step, 0)


def _conv_module(z3, conv_w, conv_b, ln_g, ln_b):
    bsz, seq, _ = z3.shape
    c = CONV_DIM
    return pl.pallas_call(
        functools.partial(_conv_body, seq=seq),
        grid=(bsz,),
        in_specs=[pl.BlockSpec((1, seq, c), lambda b: (b, 0, 0)),
                  pl.BlockSpec((1, seq, c), lambda b: (b, 0, 1)),
                  _resident((CONV_WIDTH, c)), _resident((1, c)), _resident((1, c)),
                  _resident((1, c))],
        out_specs=pl.BlockSpec((1, seq, c), lambda b: (b, 0, 0)),
        out_shape=jax.ShapeDtypeStruct((bsz, seq, c), BF16),
        scratch_shapes=[pltpu.VMEM((seq + 2 * CONV_PAD, c), F32)],
        compiler_params=_params(1),
        name="conformer_conv",
    )(z3, z3, conv_w, conv_b.reshape(1, c), ln_g.reshape(1, c), ln_b.reshape(1, c))


def _s5_body(xt_ref, par_ref, btr_ref, bti_ref, cr_ref, ci_ref, yt_ref, t_ref, w8_ref,
             *, chunk, n_chunks):
    n = SSM_STATE
    n2 = 2 * n
    p = SSM_GROUP
    r = p * chunk
    fwd = lax.broadcasted_iota(jnp.int32, (1, n2), 1) < n
    step = lax.broadcasted_iota(jnp.int32, (chunk, 1), 0).astype(F32)

    def split(a, terms):
        out = []
        for _ in range(terms):
            hi = a.astype(BF16)
            out.append(hi)
            a = a - hi.astype(F32)
        return out

    def expand(a):
        return jnp.concatenate(
            [jnp.broadcast_to(a[j:j + 1, :], (p, a.shape[1])) for j in range(chunk)], axis=0)

    def tile_rows(a):
        return jnp.concatenate([a] * chunk, axis=0)

    lr = par_ref[0, 0:1, :]
    li = par_ref[0, 1:2, :]
    dt = jnp.exp(par_ref[0, 2:3, :])
    lr_dt = lr * dt
    li_dt = li * dt

    def cpow(e):
        mag = jnp.exp(lr_dt * e)
        ang = li_dt * e
        return mag * jnp.cos(ang), mag * jnp.sin(ang)

    ar, ai = cpow(1.0)
    den = lr * lr + li * li
    pr = ar - 1.0
    coef_re = (pr * lr + ai * li) / den
    coef_im = (ai * lr - pr * li) / den
    btr = btr_ref[0]
    bti = bti_ref[0]
    bbr = tile_rows(coef_re * btr - coef_im * bti)
    bbi = tile_rows(coef_re * bti + coef_im * btr)
    ctr = tile_rows(cr_ref[0])
    cti = tile_rows(ci_ref[0])
    e_m = jnp.where(fwd, (chunk - 1.0) - step, step)
    e_q = jnp.where(fwd, step + 1.0, chunk - step)
    powers = expand(jnp.concatenate(cpow(e_m) + cpow(e_q), axis=1))
    pmr, pmi, pqr, pqi = (powers[:, i * n2:(i + 1) * n2] for i in range(4))
    mr = bbr * pmr - bbi * pmi
    mi = bbr * pmi + bbi * pmr
    qr = ctr * pqr - cti * pqi
    qi = ctr * pqi + cti * pqr

    def strip_rows(v):
        zero = jnp.zeros_like(v)
        f = jnp.where(fwd, v, zero)
        b = jnp.where(fwd, zero, v)
        return (jnp.concatenate([f, zero], axis=0)
                + jnp.concatenate([zero[:r - p], b, zero[:p]], axis=0))

    def dot_nt_split(a, b_terms):
        a_hi, a_lo = split(a, 2)
        b_hi, b_lo = b_terms
        return _dot_nt(a_hi, b_hi) + _dot_nt(a_hi, b_lo) + _dot_nt(a_lo, b_hi)

    mr_terms = split(mr, 2)
    mi_terms = split(mi, 2)
    wc = (dot_nt_split(cr_ref[0], [strip_rows(v) for v in mr_terms])
          - dot_nt_split(ci_ref[0], [strip_rows(v) for v in mi_terms]))

    width = 2 * r - V7X_LANES
    for s in range(V7X_LANES // p):
        w8_ref[s] = wc[:, p * s:p * s + width]
    for t in range(chunk):
        a, s = divmod(p * (chunk - 1 - t), V7X_LANES)
        t_ref[p * t:p * (t + 1), :] = w8_ref[s // p, :, V7X_LANES * a:V7X_LANES * a + r].astype(BF16)

    cols = xt_ref.shape[-1]
    xt = xt_ref[0].reshape(r, cols)
    y = _dot(t_ref[...], xt)

    e = _dot(jnp.concatenate([mr, mi], axis=1).T.astype(BF16), xt)
    lvl = lax.broadcasted_iota(jnp.int32, (V7X_SUBLANES, 1), 0)
    lre, lim = cpow((chunk * jnp.left_shift(1, lvl)).astype(F32))
    pcol = jnp.concatenate([lre, lim, jnp.zeros((n2 - 2 * V7X_SUBLANES, n2), F32)], axis=0).T
    jl = lax.broadcasted_iota(jnp.int32, (1, cols), 1) % n_chunks

    prev = {}
    for d in range(2):
        rows_d = slice(d * n, (d + 1) * n)
        sr, si = e[rows_d], e[n2 + d * n:n2 + (d + 1) * n]
        shifted = lambda v, s: pltpu.roll(v, s if d == 0 else cols - s, axis=1)
        k, s = 0, 1
        while s < n_chunks:
            keep = (jl >= s) if d == 0 else (jl < n_chunks - s)
            a_r = pcol[rows_d, k:k + 1]
            a_i = pcol[rows_d, V7X_SUBLANES + k:V7X_SUBLANES + k + 1]
            rr, ri = shifted(sr, s), shifted(si, s)
            sr, si = (sr + jnp.where(keep, a_r * rr - a_i * ri, 0.0),
                      si + jnp.where(keep, a_r * ri + a_i * rr, 0.0))
            k, s = k + 1, 2 * s
        keep = (jl >= 1) if d == 0 else (jl < n_chunks - 1)
        prev[d] = (jnp.where(keep, shifted(sr, 1), 0.0), jnp.where(keep, shifted(si, 1), 0.0))

    s_prev = jnp.concatenate([prev[0][0], prev[1][0], prev[0][1], prev[1][1]], axis=0)
    q_cat = jnp.concatenate([qr, -qi], axis=1)
    y = y + _dot(q_cat.astype(BF16), s_prev.astype(BF16))
    yt_ref[0] = y.reshape(chunk, p, cols).astype(yt_ref.dtype)


def _s5_scan(ut, lam_re, lam_im, log_step, b_re, b_im, c_re, c_im, n_chunks):
    g, chunk, p, cols = ut.shape
    n2 = 2 * SSM_STATE
    assert cols % V7X_LANES == 0 and n_chunks <= 2 ** V7X_SUBLANES
    r = p * chunk
    both = lambda a: a.transpose(1, 2, 0, 3).reshape(g, -1, n2)
    par = both(jnp.stack([lam_re, lam_im, jnp.broadcast_to(log_step[..., None], lam_re.shape)],
                         axis=2))
    bt_re = both(b_re.transpose(0, 1, 3, 2))
    bt_im = both(b_im.transpose(0, 1, 3, 2))
    grp = lambda *tail: pl.BlockSpec((1,) + tail, lambda i: (i,) + (0,) * len(tail))
    return pl.pallas_call(
        functools.partial(_s5_body, chunk=chunk, n_chunks=n_chunks),
        grid=(g,),
        in_specs=[grp(chunk, p, cols), grp(3, n2), grp(p, n2), grp(p, n2), grp(p, n2), grp(p, n2)],
        out_specs=grp(chunk, p, cols),
        out_shape=jax.ShapeDtypeStruct(ut.shape, BF16),
        scratch_shapes=[pltpu.VMEM((r, r), BF16),
                        pltpu.VMEM((V7X_LANES // p, p, 2 * r - V7X_LANES), F32)],
        compiler_params=_params(1),
        name="s5_chunked",
    )(ut, par, bt_re, bt_im, both(c_re), both(c_im))


def _na_body(q_ref, k_ref, v_ref, qg_ref, kg_ref, bias_ref, mask_ref, o_ref,
             qq_ref, kw_ref, vw_ref, sa_ref, sb_ref, tab_ref, *, rows):
    w = GRID_W
    pair = 2 * NA_HEAD_DIM
    head0 = lax.broadcasted_iota(jnp.int32, (1, pair), 1) < NA_HEAD_DIM
    head_mean = jnp.where(lax.broadcasted_iota(jnp.int32, (pair, pair), 0) // NA_HEAD_DIM
                          == lax.broadcasted_iota(jnp.int32, (pair, pair), 1) // NA_HEAD_DIM,
                          1.0 / NA_HEAD_DIM, 0.0).astype(BF16)

    def head_norm(t, gain):
        sq = t * t
        hi = sq.astype(BF16)
        lo = (sq - hi.astype(F32)).astype(BF16)
        ms = _dot(hi, head_mean) + _dot(lo, head_mean)
        return t * lax.rsqrt(ms + NORM_EPS) * gain

    qn = head_norm(q_ref[0].astype(F32), qg_ref[...])
    qh = (jnp.where(head0, qn, 0.0), jnp.where(head0, 0.0, qn))
    kn = head_norm(k_ref[0].astype(F32), kg_ref[...]).astype(BF16)

    cb = NA_WIN_COLS
    sub = V7X_SUBLANES
    n_blocks = w // cb
    lanes = NA_WIN_ROWS * cb
    live = _na_live_tiles()

    @pl.when(pl.program_id(1) == 0)
    def _():
        for h in range(2):
            for a in range(w // sub):
                qs = slice(a * sub, (a + 1) * sub)
                for t in live[a]:
                    src = bias_ref[0, h, t, qs, :]
                    for i in range(NA_WIN_ROWS):
                        tab_ref[h, i, qs, t * lanes:(t + 1) * lanes] = src[:, i * cb:i * cb + lanes]

    vv = v_ref[0]
    for row in range(rows):
        for t in range(n_blocks):
            src = slice(row * w + t * cb, row * w + (t + 1) * cb)
            kw_ref[t, row * cb:(row + 1) * cb, :] = kn[src]
            vw_ref[t, row * cb:(row + 1) * cb, :] = vv[src]
        stacked = jnp.concatenate([t[row * w:(row + 1) * w] for t in qh], axis=0)
        qq_ref[2 * row * w:2 * (row + 1) * w, :] = stacked.astype(BF16)

    n_steps = rows // NA_ROWS_PER_STEP

    def offsets(step, j):
        row = step * NA_ROWS_PER_STEP + j
        start = jnp.clip(row - NA_WIN_ROWS // 2, 0, rows - NA_WIN_ROWS)
        return (row, pl.multiple_of(start * cb, cb), start - row + (NA_WIN_ROWS - 1))

    def window(ref, k_off):
        return jnp.concatenate([ref[t, pl.ds(k_off, lanes), :] for t in range(n_blocks)], axis=0)

    def score_row(step, j, stage_ref):
        row, k_off, _ = offsets(step, j)
        q_off = pl.multiple_of(row * (2 * w), 2 * w)
        stage_ref[j] = _dot_nt(qq_ref[pl.ds(q_off, 2 * w), :], window(kw_ref, k_off))

    def value_row(step, j, stage_ref):
        row, k_off, bias_row = offsets(step, j)
        p_rows, l_rows = [], []
        for h in range(2):
            for a in range(w // sub):
                qs = slice(a * sub, (a + 1) * sub)
                tiles = {}
                for t in live[a]:
                    ks = slice(t * lanes, (t + 1) * lanes)
                    s = stage_ref[j, h * w + a * sub:h * w + (a + 1) * sub, ks]
                    s = s + tab_ref[h, bias_row, qs, ks]
                    tiles[t] = jnp.where(mask_ref[qs, ks] != 0, s, MASK_VALUE)
                m = functools.reduce(jnp.maximum, tiles.values())
                m = jnp.maximum(jnp.max(m, axis=-1, keepdims=True), MASK_VALUE)
                tiles = {t: jnp.exp(s - m) for t, s in tiles.items()}
                masked = jnp.exp(MASK_VALUE - m)
                n_masked = (n_blocks - len(tiles)) * lanes
                l_rows.append(jnp.sum(functools.reduce(jnp.add, tiles.values()), axis=-1,
                                      keepdims=True) + n_masked * masked)
                fill = jnp.broadcast_to(masked, (sub, lanes))
                p_rows.append(jnp.concatenate([tiles.get(t, fill) for t in range(n_blocks)], axis=1))
        p = jnp.concatenate(p_rows, axis=0).astype(BF16)
        o = _dot(p, window(vw_ref, k_off)) / jnp.concatenate(l_rows, axis=0)
        q_off = pl.multiple_of(row * w, w)
        o_ref[0, pl.ds(q_off, w), :] = jnp.where(head0, o[:w], o[w:]).astype(o_ref.dtype)

    for j in range(NA_ROWS_PER_STEP):
        score_row(0, j, sa_ref)

    def step_pair(i, carry):
        even = 2 * i
        for j in range(NA_ROWS_PER_STEP):
            score_row(even + 1, j, sb_ref)
            value_row(even, j, sa_ref)
        nxt = jnp.minimum(even + 2, n_steps - 1)
        for j in range(NA_ROWS_PER_STEP):
            score_row(nxt, j, sa_ref)
            value_row(even + 1, j, sb_ref)
        return carry

    lax.fori_loop(0, n_steps // 2, step_pair, 0)


def _na_col_start():
    cols = np.arange(GRID_W)
    return np.clip(cols - NA_WIN_COLS // 2, 0, GRID_W - NA_WIN_COLS)


def _na_col_valid():
    cols = np.arange(GRID_W)
    cs = _na_col_start()[:, None]
    return (cols[None, :] >= cs) & (cols[None, :] < cs + NA_WIN_COLS)


def _na_live_tiles():
    valid = _na_col_valid().reshape(GRID_W // V7X_SUBLANES, V7X_SUBLANES,
                                    GRID_W // NA_WIN_COLS, NA_WIN_COLS)
    return [list(np.nonzero(tile.any(axis=(0, 2)))[0]) for tile in valid]


def _na_bias_rows(rpb):
    w, cb = GRID_W, NA_WIN_COLS
    edge = w - cb
    ext = jnp.concatenate([jnp.repeat(rpb[..., :1], edge, axis=-1), rpb,
                           jnp.repeat(rpb[..., -1:], edge, axis=-1)], axis=-1)
    ext = jnp.pad(ext, ((0, 0), (0, 1), (0, 0)))
    pieces = [ext[:, :, t * cb - q + w - 1:t * cb - q + w - 1 + cb]
              for t in range(w // cb) for q in range(w)]
    return jnp.stack(pieces, axis=1).reshape(NA_HEADS // 2, 2, w // cb, w, 2 * NA_WIN_ROWS * cb)


def _na_col_mask():
    w, cb = GRID_W, NA_WIN_COLS
    m = _na_col_valid().reshape(w, w // cb, 1, cb)
    m = np.broadcast_to(m, (w, w // cb, NA_WIN_ROWS, cb))
    return jnp.asarray(m.reshape(w, NA_WIN_ROWS * w).astype(np.int32))


def _neighbourhood_attention(qkv3, q_gain, k_gain, rpb):
    bsz, seq, d3 = qkv3.shape
    d = d3 // 3
    rows = seq // GRID_W
    assert rows >= NA_WIN_ROWS and rows % (2 * NA_ROWS_PER_STEP) == 0 and d == NA_HEADS * NA_HEAD_DIM
    pair = 2 * NA_HEAD_DIM
    n_pairs = d // pair
    keys = NA_WIN_ROWS * GRID_W
    assert NA_WIN_ROWS * NA_WIN_COLS == V7X_LANES and GRID_W % NA_WIN_COLS == 0
    blk = lambda off: pl.BlockSpec((1, seq, pair), lambda hp, b: (b, 0, off + hp))
    gains = lambda gvec: jnp.tile(gvec, 2).reshape(1, pair)
    stage = pltpu.VMEM((NA_ROWS_PER_STEP, 2 * GRID_W, keys), F32)
    windows = pltpu.VMEM((GRID_W // NA_WIN_COLS, rows * NA_WIN_COLS, pair), BF16)
    queries = pltpu.VMEM((2 * seq, pair), BF16)
    return pl.pallas_call(
        functools.partial(_na_body, rows=rows),
        grid=(n_pairs, bsz),
        in_specs=[blk(0), blk(n_pairs), blk(2 * n_pairs), _resident((1, pair)), _resident((1, pair)),
                  pl.BlockSpec((1, 2, GRID_W // NA_WIN_COLS, GRID_W, 2 * NA_WIN_ROWS * NA_WIN_COLS),
                               lambda hp, b: (hp, 0, 0, 0, 0)),
                  _resident((GRID_W, keys))],
        out_specs=pl.BlockSpec((1, seq, pair), lambda hp, b: (b, 0, hp)),
        out_shape=jax.ShapeDtypeStruct((bsz, seq, d), BF16),
        scratch_shapes=[queries] + [windows] * 2 + [stage] * 2
        + [pltpu.VMEM((2, NA_WIN_ROWS, GRID_W, keys), F32)],
        compiler_params=_params(2),
        name="neighbourhood_attention",
    )(qkv3, qkv3, qkv3, gains(q_gain) * (NA_HEAD_DIM ** -0.5), gains(k_gain),
      _na_bias_rows(rpb), _na_col_mask())


def kernel(x, mix_norm, ffn_norm, ffn_w_gate, ffn_w_up, ffn_w_down, ab_w_in, conv_w, conv_b, conv_ln_g, conv_ln_b, ssm_lambda_re, ssm_lambda_im, ssm_log_step, ssm_b_re, ssm_b_im, ssm_c_re, ssm_c_im, ssm_d, ssm_glu_w, ssm_glu_b, ab_w_out, na_w_qkv, na_q_norm, na_k_norm, na_rpb, na_w_out):
    bsz, seq, d = x.shape
    m = bsz * seq
    chunk = S5_CHUNK
    assert m % ROW_TILE == 0 and seq % chunk == 0
    n_chunks = seq // chunk
    rows_c = bsz * n_chunks
    depth = mix_norm.shape[0]
    bf = lambda w: w.astype(BF16)
    xs = x.reshape(m, d)
    for layer in range(depth):
        i = layer // 2
        ffn = (ffn_norm[layer], bf(ffn_w_gate[layer]), bf(ffn_w_up[layer]), bf(ffn_w_down[layer]))
        if layer % 2 == 0:
            xc = xs.reshape(rows_c, chunk, d)
            z, ut = _ssm_in_proj(xc, mix_norm[layer], bf(ab_w_in[i]))
            ya = _conv_module(z.reshape(bsz, seq, -1), conv_w[i], conv_b[i], conv_ln_g[i],
                              conv_ln_b[i])
            yt = _s5_scan(ut, ssm_lambda_re[i], ssm_lambda_im[i], ssm_log_step[i], ssm_b_re[i],
                          ssm_b_im[i], ssm_c_re[i], ssm_c_im[i], n_chunks)
            ys = _s5_unlayout(yt)
            xs = _ssm_ffn(xs, ya.reshape(m, -1), ys.reshape(m, -1), z.reshape(m, -1), ssm_d[i],
                          bf(ssm_glu_w[i]), ssm_glu_b[i], bf(ab_w_out[i]), *ffn)
        else:
            qkv = _norm_matmul(xs, mix_norm[layer], bf(na_w_qkv[i]))
            attn = _neighbourhood_attention(qkv.reshape(bsz, seq, -1), na_q_norm[i], na_k_norm[i],
                                            na_rpb[i])
            xs = _attn_ffn(xs, attn.reshape(m, d), bf(na_w_out[i]), *ffn)
    return xs.reshape(bsz, seq, d)
```

```python
import functools

import numpy as np
import jax
import jax.numpy as jnp
from jax import lax
from jax.experimental import pallas as pl
from jax.experimental.pallas import tpu as pltpu

F32 = jnp.float32
BF16 = jnp.bfloat16

GRID_W = 64
CONV_DIM = 512
CONV_WIDTH = 31
SSM_DIM = 512
SSM_GROUP = 16
SSM_GROUPS = SSM_DIM // SSM_GROUP
SSM_STATE = 64
NA_HEADS = 16
NA_HEAD_DIM = 64
NA_WIN_ROWS = 8
NA_WIN_COLS = 16
NORM_EPS = 1e-6
LN_EPS = 1e-5
MASK_VALUE = -1e30

S5_CHUNK = 64
S5_STEP_BLOCK = 8
S5_CHUNK_BLOCK = 128
V7X_LANES = 128
V7X_SUBLANES = 8
V7X_VMEM_LIMIT = 56 * 1024 * 1024
ROW_TILE = 512
CONV_TILE = 256
CONV_PAD = 16
NA_ROWS_PER_STEP = 4


def _params(n_axes):
    return pltpu.CompilerParams(dimension_semantics=("arbitrary",) * n_axes,
                                vmem_limit_bytes=V7X_VMEM_LIMIT)


def _resident(shape):
    return pl.BlockSpec(shape, lambda *_: (0,) * len(shape), pipeline_mode=pl.Buffered(1))


def _rms_norm(x, gain):
    ms = jnp.mean(x * x, axis=-1, keepdims=True)
    return x * lax.rsqrt(ms + NORM_EPS) * gain


def _dot(a, b):
    return jnp.dot(a, b, preferred_element_type=F32)


def _dot_nt(a, b):
    return lax.dot_general(a, b, (((1,), (1,)), ((), ())), preferred_element_type=F32)


def _qkv_proj_body(x_ref, g_ref, w_ref, qg_ref, kg_ref, o_ref):
    h = _rms_norm(x_ref[...], g_ref[...]).astype(BF16)
    d = x_ref.shape[1]
    pair = 2 * NA_HEAD_DIM
    head0 = lax.broadcasted_iota(jnp.int32, (1, pair), 1) < NA_HEAD_DIM
    for part, gain_ref in enumerate((qg_ref, kg_ref)):
        t = _dot(h, w_ref[:, part * d:(part + 1) * d])
        for l in range(d // pair):
            cols = slice(l * pair, (l + 1) * pair)
            tl = t[:, cols]
            sq = tl * tl
            both = jnp.sum(sq, axis=-1, keepdims=True)
            first = jnp.sum(jnp.where(head0, sq, 0.0), axis=-1, keepdims=True)
            ms = jnp.where(head0, first, both - first) / NA_HEAD_DIM
            o_ref[:, part * d + l * pair:part * d + (l + 1) * pair] = (
                tl * lax.rsqrt(ms + NORM_EPS) * gain_ref[:, cols]).astype(o_ref.dtype)
    o_ref[:, 2 * d:] = _dot(h, w_ref[:, 2 * d:]).astype(o_ref.dtype)


def _qkv_proj(x, gain, w, q_gain, k_gain):
    m, d = x.shape
    n = w.shape[1]
    heads = d // NA_HEAD_DIM
    qg = jnp.tile(q_gain, heads).reshape(1, d) * (NA_HEAD_DIM ** -0.5)
    kg = jnp.tile(k_gain, heads).reshape(1, d)
    return pl.pallas_call(
        _qkv_proj_body,
        grid=(m // ROW_TILE,),
        in_specs=[pl.BlockSpec((ROW_TILE, d), lambda i: (i, 0)),
                  _resident((1, d)), _resident((d, n)), _resident((1, d)), _resident((1, d))],
        out_specs=pl.BlockSpec((ROW_TILE, n), lambda i: (i, 0)),
        out_shape=jax.ShapeDtypeStruct((m, n), BF16),
        compiler_params=_params(1),
        name="qkv_proj",
    )(x, gain.reshape(1, d), w, qg, kg)


def _ffn_tail(x1, fg_ref, wg_ref, wu_ref, wd_ref):
    h = _rms_norm(x1, fg_ref[...]).astype(BF16)
    g = _dot(h, wg_ref[...])
    u = _dot(h, wu_ref[...])
    a = (g * jax.nn.sigmoid(g) * u).astype(BF16)
    return x1 + _dot(a, wd_ref[...])


def _attn_ffn_body(x_ref, a_ref, wo_ref, fg_ref, wg_ref, wu_ref, wd_ref, o_ref):
    x1 = x_ref[...] + _dot(a_ref[...], wo_ref[...])
    o_ref[...] = _ffn_tail(x1, fg_ref, wg_ref, wu_ref, wd_ref)


def _ssm_ffn_body(x_ref, ya_ref, ys_ref, u_ref, d_ref, gw_ref, gb_ref, wo_ref,
                  fg_ref, wg_ref, wu_ref, wd_ref, o_ref):
    y = ys_ref[...] + d_ref[...] * u_ref[...]
    y = jax.nn.gelu(y, approximate=True)
    gate = jax.nn.sigmoid(_dot(y.astype(BF16), gw_ref[...]) + gb_ref[...])
    yb = (y * gate).astype(BF16)
    x1 = (x_ref[...] + _dot(ya_ref[...], wo_ref[:CONV_DIM, :]) + _dot(yb, wo_ref[CONV_DIM:, :]))
    o_ref[...] = _ffn_tail(x1, fg_ref, wg_ref, wu_ref, wd_ref)


def _row_spec(width, col_block=0):
    return pl.BlockSpec((ROW_TILE, width), lambda i: (i, col_block))


def _attn_ffn(x, attn, w_out, ffn_gain, w_gate, w_up, w_down):
    m, d = x.shape
    f = w_gate.shape[1]
    return pl.pallas_call(
        _attn_ffn_body,
        grid=(m // ROW_TILE,),
        in_specs=[_row_spec(d), _row_spec(d), _resident((d, d)), _resident((1, d)),
                  _resident((d, f)), _resident((d, f)), _resident((f, d))],
        out_specs=_row_spec(d),
        out_shape=jax.ShapeDtypeStruct((m, d), F32),
        compiler_params=_params(1),
        name="attn_out_ffn",
    )(x, attn, w_out, ffn_gain.reshape(1, d), w_gate, w_up, w_down)


def _ssm_ffn(x, ya, ys, z, d_skip, glu_w, glu_b, w_out, ffn_gain, w_gate, w_up, w_down):
    m, d = x.shape
    f = w_gate.shape[1]
    c = SSM_DIM
    return pl.pallas_call(
        _ssm_ffn_body,
        grid=(m // ROW_TILE,),
        in_specs=[_row_spec(d), _row_spec(CONV_DIM), _row_spec(c),
                  _row_spec(c, col_block=2 * CONV_DIM // c),
                  _resident((1, c)), _resident((c, c)), _resident((1, c)),
                  _resident((CONV_DIM + c, d)), _resident((1, d)),
                  _resident((d, f)), _resident((d, f)), _resident((f, d))],
        out_specs=_row_spec(d),
        out_shape=jax.ShapeDtypeStruct((m, d), F32),
        compiler_params=_params(1),
        name="ssm_out_ffn",
    )(x, ya, ys, z, d_skip.reshape(1, c), glu_w, glu_b.reshape(1, c), w_out,
      ffn_gain.reshape(1, d), w_gate, w_up, w_down)


def _token_block_spec(width):
    return pl.BlockSpec((S5_CHUNK_BLOCK, S5_STEP_BLOCK, width), lambda a, cb: (cb, a, 0))


def _slab_block_spec():
    return pl.BlockSpec((SSM_GROUPS, S5_STEP_BLOCK, SSM_GROUP, S5_CHUNK_BLOCK),
                        lambda a, cb: (0, a, 0, cb))


def _ssm_in_proj_body(x_ref, g_ref, w_ref, z_ref, ut_ref, u_scr):
    cb, st, d = x_ref.shape
    h = _rms_norm(x_ref[...].reshape(cb * st, d), g_ref[...]).astype(BF16)
    z = _dot(h, w_ref[...])
    z_ref[...] = z.reshape(cb, st, -1)
    lanes = V7X_LANES
    groups = lanes // SSM_GROUP
    for l in range(SSM_DIM // lanes):
        u_scr[l] = z[:, 2 * CONV_DIM + l * lanes:2 * CONV_DIM + (l + 1) * lanes]
        for t in range(st):
            ut = u_scr[l, pl.ds(t, cb, stride=st), :].T
            ut_ref[l * groups:(l + 1) * groups, t] = (
                ut.reshape(groups, SSM_GROUP, cb).astype(ut_ref.dtype))


def _ssm_in_proj(xc, gain, w):
    rows, chunk, d = xc.shape
    n = w.shape[1]
    assert rows % S5_CHUNK_BLOCK == 0 and chunk % S5_STEP_BLOCK == 0
    return pl.pallas_call(
        _ssm_in_proj_body,
        grid=(chunk // S5_STEP_BLOCK, rows // S5_CHUNK_BLOCK),
        in_specs=[_token_block_spec(d), _resident((1, d)), _resident((d, n))],
        out_specs=[_token_block_spec(n), _slab_block_spec()],
        out_shape=[jax.ShapeDtypeStruct((rows, chunk, n), F32),
                   jax.ShapeDtypeStruct((SSM_GROUPS, chunk, SSM_GROUP, rows), BF16)],
        scratch_shapes=[pltpu.VMEM((SSM_DIM // V7X_LANES, S5_CHUNK_BLOCK * S5_STEP_BLOCK, V7X_LANES),
                                   F32)],
        compiler_params=_params(2),
        name="ssm_in_proj",
    )(xc, gain.reshape(1, d), w)


def _s5_unlayout_body(yt_ref, ys_ref, y_scr):
    g, st, p, cb = yt_ref.shape
    lanes = V7X_LANES
    groups = lanes // p
    for l in range(g * p // lanes):
        for t in range(st):
            y = yt_ref[l * groups:(l + 1) * groups, t].astype(F32).reshape(lanes, cb).T
            y_scr[l, pl.ds(t, cb, stride=st), :] = y
        ys_ref[:, :, l * lanes:(l + 1) * lanes] = y_scr[l].reshape(cb, st, lanes)


def _s5_unlayout(yt):
    g, chunk, p, rows = yt.shape
    return pl.pallas_call(
        _s5_unlayout_body,
        grid=(chunk // S5_STEP_BLOCK, rows // S5_CHUNK_BLOCK),
        in_specs=[_slab_block_spec()],
        out_specs=_token_block_spec(g * p),
        out_shape=jax.ShapeDtypeStruct((rows, chunk, g * p), F32),
        scratch_shapes=[pltpu.VMEM((g * p // V7X_LANES, S5_CHUNK_BLOCK * S5_STEP_BLOCK, V7X_LANES),
                                   F32)],
        compiler_params=_params(2),
        name="s5_unlayout",
    )(yt)


def _conv_body(za_ref, zg_ref, w_ref, cb_ref, lg_ref, lb_ref, o_ref, apad_ref, *, seq):
    c = CONV_DIM
    apad_ref[0:CONV_PAD, :] = jnp.zeros((CONV_PAD, c), F32)
    apad_ref[CONV_PAD + seq:, :] = jnp.zeros((CONV_PAD, c), F32)

    glu_tile = 256

    def glu_step(i, carry):
        r0 = pl.multiple_of(i * glu_tile, glu_tile)
        za = za_ref[0, pl.ds(r0, glu_tile), :].astype(F32)
        zg = zg_ref[0, pl.ds(r0, glu_tile), :].astype(F32)
        apad_ref[pl.ds(CONV_PAD + r0, glu_tile), :] = za * jax.nn.sigmoid(zg)
        return carry

    lax.fori_loop(0, seq // glu_tile, glu_step, 0)

    first = CONV_PAD - CONV_WIDTH // 2
    sub = V7X_SUBLANES

    def conv_step(i, carry):
        r0 = pl.multiple_of(i * CONV_TILE, CONV_TILE)
        win = apad_ref[pl.ds(r0, CONV_TILE + 2 * CONV_PAD), :]
        acc = cb_ref[...]
        for r in range(sub):
            part = None
            for k in range(r, CONV_WIDTH, sub):
                term = w_ref[k:k + 1, :] * win[k - r:k - r + CONV_TILE + sub]
                part = term if part is None else part + term
            acc = acc + part[first + r:first + r + CONV_TILE]
        mu = jnp.mean(acc, axis=-1, keepdims=True)
        xc = acc - mu
        var = jnp.mean(xc * xc, axis=-1, keepdims=True)
        y = xc * lax.rsqrt(var + LN_EPS) * lg_ref[...] + lb_ref[...]
        o_ref[0, pl.ds(r0, CONV_TILE), :] = (y * jax.nn.sigmoid(y)).astype(o_ref.dtype)
        return carry

    lax.fori_loop(0, seq // CONV_TILE, conv_step, 0)


def _conv_module(z3, conv_w, conv_b, ln_g, ln_b):
    bsz, seq, _ = z3.shape
    c = CONV_DIM
    return pl.pallas_call(
        functools.partial(_conv_body, seq=seq),
        grid=(bsz,),
        in_specs=[pl.BlockSpec((1, seq, c), lambda b: (b, 0, 0)),
                  pl.BlockSpec((1, seq, c), lambda b: (b, 0, 1)),
                  _resident((CONV_WIDTH, c)), _resident((1, c)), _resident((1, c)),
                  _resident((1, c))],
        out_specs=pl.BlockSpec((1, seq, c), lambda b: (b, 0, 0)),
        out_shape=jax.ShapeDtypeStruct((bsz, seq, c), BF16),
        scratch_shapes=[pltpu.VMEM((seq + 2 * CONV_PAD, c), F32)],
        compiler_params=_params(1),
        name="conformer_conv",
    )(z3, z3, conv_w, conv_b.reshape(1, c), ln_g.reshape(1, c), ln_b.reshape(1, c))


def _s5_body(xt_ref, par_ref, btr_ref, bti_ref, cr_ref, ci_ref, yt_ref, t_ref, w8_ref,
             *, chunk, n_chunks):
    n = SSM_STATE
    n2 = 2 * n
    p = SSM_GROUP
    r = p * chunk
    fwd = lax.broadcasted_iota(jnp.int32, (1, n2), 1) < n
    step = lax.broadcasted_iota(jnp.int32, (chunk, 1), 0).astype(F32)

    def split(a, terms):
        out = []
        for _ in range(terms):
            hi = a.astype(BF16)
            out.append(hi)
            a = a - hi.astype(F32)
        return out

    def expand(a):
        return jnp.concatenate(
            [jnp.broadcast_to(a[j:j + 1, :], (p, a.shape[1])) for j in range(chunk)], axis=0)

    def tile_rows(a):
        return jnp.concatenate([a] * chunk, axis=0)

    lr = par_ref[0, 0:1, :]
    li = par_ref[0, 1:2, :]
    dt = jnp.exp(par_ref[0, 2:3, :])
    lr_dt = lr * dt
    li_dt = li * dt

    def cpow(e):
        mag = jnp.exp(lr_dt * e)
        ang = li_dt * e
        return mag * jnp.cos(ang), mag * jnp.sin(ang)

    ar, ai = cpow(1.0)
    den = lr * lr + li * li
    pr = ar - 1.0
    coef_re = (pr * lr + ai * li) / den
    coef_im = (ai * lr - pr * li) / den
    btr = btr_ref[0]
    bti = bti_ref[0]
    bbr = tile_rows(coef_re * btr - coef_im * bti)
    bbi = tile_rows(coef_re * bti + coef_im * btr)
    ctr = tile_rows(cr_ref[0])
    cti = tile_rows(ci_ref[0])
    e_m = jnp.where(fwd, (chunk - 1.0) - step, step)
    e_q = jnp.where(fwd, step + 1.0, chunk - step)
    powers = expand(jnp.concatenate(cpow(e_m) + cpow(e_q), axis=1))
    pmr, pmi, pqr, pqi = (powers[:, i * n2:(i + 1) * n2] for i in range(4))
    mr = bbr * pmr - bbi * pmi
    mi = bbr * pmi + bbi * pmr
    qr = ctr * pqr - cti * pqi
    qi = ctr * pqi + cti * pqr

    def strip_rows(v):
        zero = jnp.zeros_like(v)
        f = jnp.where(fwd, v, zero)
        b = jnp.where(fwd, zero, v)
        return (jnp.concatenate([f, zero], axis=0)
                + jnp.concatenate([zero[:r - p], b, zero[:p]], axis=0))

    def dot_nt_split(a, b_terms):
        a_hi, a_lo = split(a, 2)
        b_hi, b_lo = b_terms
        return _dot_nt(a_hi, b_hi) + _dot_nt(a_hi, b_lo) + _dot_nt(a_lo, b_hi)

    mr_terms = split(mr, 2)
    mi_terms = split(mi, 2)
    wc = (dot_nt_split(cr_ref[0], [strip_rows(v) for v in mr_terms])
          - dot_nt_split(ci_ref[0], [strip_rows(v) for v in mi_terms]))

    width = 2 * r - V7X_LANES
    for s in range(V7X_LANES // p):
        w8_ref[s] = wc[:, p * s:p * s + width]
    for t in range(chunk):
        a, s = divmod(p * (chunk - 1 - t), V7X_LANES)
        t_ref[p * t:p * (t + 1), :] = w8_ref[s // p, :, V7X_LANES * a:V7X_LANES * a + r].astype(BF16)

    cols = xt_ref.shape[-1]
    xt = xt_ref[0].reshape(r, cols)
    y = _dot(t_ref[...], xt)

    e = _dot(jnp.concatenate([mr, mi], axis=1).T.astype(BF16), xt)
    lvl = lax.broadcasted_iota(jnp.int32, (V7X_SUBLANES, 1), 0)
    lre, lim = cpow((chunk * jnp.left_shift(1, lvl)).astype(F32))
    pcol = jnp.concatenate([lre, lim, jnp.zeros((n2 - 2 * V7X_SUBLANES, n2), F32)], axis=0).T
    jl = lax.broadcasted_iota(jnp.int32, (1, cols), 1) % n_chunks

    prev = {}
    for d in range(2):
        rows_d = slice(d * n, (d + 1) * n)
        sr, si = e[rows_d], e[n2 + d * n:n2 + (d + 1) * n]
        shifted = lambda v, s: pltpu.roll(v, s if d == 0 else cols - s, axis=1)
        k, s = 0, 1
        while s < n_chunks:
            keep = (jl >= s) if d == 0 else (jl < n_chunks - s)
            a_r = pcol[rows_d, k:k + 1]
            a_i = pcol[rows_d, V7X_SUBLANES + k:V7X_SUBLANES + k + 1]
            rr, ri = shifted(sr, s), shifted(si, s)
            sr, si = (sr + jnp.where(keep, a_r * rr - a_i * ri, 0.0),
                      si + jnp.where(keep, a_r * ri + a_i * rr, 0.0))
            k, s = k + 1, 2 * s
        keep = (jl >= 1) if d == 0 else (jl < n_chunks - 1)
        prev[d] = (jnp.where(keep, shifted(sr, 1), 0.0), jnp.where(keep, shifted(si, 1), 0.0))

    s_prev = jnp.concatenate([prev[0][0], prev[1][0], prev[0][1], prev[1][1]], axis=0)
    q_cat = jnp.concatenate([qr, -qi], axis=1)
    y = y + _dot(q_cat.astype(BF16), s_prev.astype(BF16))
    yt_ref[0] = y.reshape(chunk, p, cols).astype(yt_ref.dtype)


def _s5_scan(ut, lam_re, lam_im, log_step, b_re, b_im, c_re, c_im, n_chunks):
    g, chunk, p, cols = ut.shape
    n2 = 2 * SSM_STATE
    assert cols % V7X_LANES == 0 and n_chunks <= 2 ** V7X_SUBLANES
    r = p * chunk
    both = lambda a: a.transpose(1, 2, 0, 3).reshape(g, -1, n2)
    par = both(jnp.stack([lam_re, lam_im, jnp.broadcast_to(log_step[..., None], lam_re.shape)],
                         axis=2))
    bt_re = both(b_re.transpose(0, 1, 3, 2))
    bt_im = both(b_im.transpose(0, 1, 3, 2))
    grp = lambda *tail: pl.BlockSpec((1,) + tail, lambda i: (i,) + (0,) * len(tail))
    return pl.pallas_call(
        functools.partial(_s5_body, chunk=chunk, n_chunks=n_chunks),
        grid=(g,),
        in_specs=[grp(chunk, p, cols), grp(3, n2), grp(p, n2), grp(p, n2), grp(p, n2), grp(p, n2)],
        out_specs=grp(chunk, p, cols),
        out_shape=jax.ShapeDtypeStruct(ut.shape, BF16),
        scratch_shapes=[pltpu.VMEM((r, r), BF16),
                        pltpu.VMEM((V7X_LANES // p, p, 2 * r - V7X_LANES), F32)],
        compiler_params=_params(1),
        name="s5_chunked",
    )(ut, par, bt_re, bt_im, both(c_re), both(c_im))


def _na_body(q_ref, k_ref, v_ref, tab_ref, mask_ref, o_ref,
             qq_ref, kw_ref, vw_ref, sa_ref, sb_ref, *, rows):
    w = GRID_W
    pair = 2 * NA_HEAD_DIM
    head0 = lax.broadcasted_iota(jnp.int32, (1, pair), 1) < NA_HEAD_DIM
    qn = q_ref[0]
    qh = (jnp.where(head0, qn, jnp.zeros_like(qn)), jnp.where(head0, jnp.zeros_like(qn), qn))
    kn = k_ref[0]

    cb = NA_WIN_COLS
    n_blocks = w // cb
    lanes = NA_WIN_ROWS * cb
    live = _na_live_tiles()
    vv = v_ref[0]
    for row in range(rows):
        for t in range(n_blocks):
            src = slice(row * w + t * cb, row * w + (t + 1) * cb)
            kw_ref[t, row * cb:(row + 1) * cb, :] = kn[src]
            vw_ref[t, row * cb:(row + 1) * cb, :] = vv[src]
        stacked = jnp.concatenate([t[row * w:(row + 1) * w] for t in qh], axis=0)
        qq_ref[2 * row * w:2 * (row + 1) * w, :] = stacked

    n_steps = rows // NA_ROWS_PER_STEP
    sub = V7X_SUBLANES

    def offsets(step, j):
        row = step * NA_ROWS_PER_STEP + j
        start = jnp.clip(row - NA_WIN_ROWS // 2, 0, rows - NA_WIN_ROWS)
        return (row, pl.multiple_of(start * cb, cb), start - row + (NA_WIN_ROWS - 1))

    def window(ref, k_off):
        return jnp.concatenate([ref[t, pl.ds(k_off, lanes), :] for t in range(n_blocks)], axis=0)

    def score_row(step, j, stage_ref):
        row, k_off, _ = offsets(step, j)
        q_off = pl.multiple_of(row * (2 * w), 2 * w)
        stage_ref[j] = _dot_nt(qq_ref[pl.ds(q_off, 2 * w), :], window(kw_ref, k_off))

    def value_row(step, j, stage_ref):
        row, k_off, bias_row = offsets(step, j)
        p_rows, l_rows = [], []
        for h in range(2):
            for a in range(w // sub):
                qs = slice(a * sub, (a + 1) * sub)
                tiles = {}
                for t in live[a]:
                    ks = slice(t * lanes, (t + 1) * lanes)
                    s = stage_ref[j, h * w + a * sub:h * w + (a + 1) * sub, ks]
                    s = s + tab_ref[0, h, bias_row, qs, ks]
                    tiles[t] = jnp.where(mask_ref[qs, ks] != 0, s, MASK_VALUE)
                m = functools.reduce(jnp.maximum, tiles.values())
                m = jnp.maximum(jnp.max(m, axis=-1, keepdims=True), MASK_VALUE)
                tiles = {t: jnp.exp(s - m) for t, s in tiles.items()}
                masked = jnp.exp(MASK_VALUE - m)
                n_masked = (n_blocks - len(tiles)) * lanes
                l_rows.append(jnp.sum(functools.reduce(jnp.add, tiles.values()), axis=-1,
                                      keepdims=True) + n_masked * masked)
                fill = jnp.broadcast_to(masked, (sub, lanes))
                p_rows.append(jnp.concatenate([tiles.get(t, fill) for t in range(n_blocks)], axis=1))
        p = jnp.concatenate(p_rows, axis=0).astype(BF16)
        o = _dot(p, window(vw_ref, k_off)) / jnp.concatenate(l_rows, axis=0)
        q_off = pl.multiple_of(row * w, w)
        o_ref[0, pl.ds(q_off, w), :] = jnp.where(head0, o[:w], o[w:]).astype(o_ref.dtype)

    for j in range(NA_ROWS_PER_STEP):
        score_row(0, j, sa_ref)

    def step_pair(i, carry):
        even = 2 * i
        for j in range(NA_ROWS_PER_STEP):
            score_row(even + 1, j, sb_ref)
            value_row(even, j, sa_ref)
        nxt = jnp.minimum(even + 2, n_steps - 1)
        for j in range(NA_ROWS_PER_STEP):
            score_row(nxt, j, sa_ref)
            value_row(even + 1, j, sb_ref)
        return carry

    lax.fori_loop(0, n_steps // 2, step_pair, 0)


def _na_col_start():
    cols = np.arange(GRID_W)
    return np.clip(cols - NA_WIN_COLS // 2, 0, GRID_W - NA_WIN_COLS)


def _na_col_valid():
    cols = np.arange(GRID_W)
    cs = _na_col_start()[:, None]
    return (cols[None, :] >= cs) & (cols[None, :] < cs + NA_WIN_COLS)


def _na_live_tiles():
    valid = _na_col_valid().reshape(GRID_W // V7X_SUBLANES, V7X_SUBLANES,
                                    GRID_W // NA_WIN_COLS, NA_WIN_COLS)
    return [list(np.nonzero(tile.any(axis=(0, 2)))[0]) for tile in valid]


def _na_bias_tables(rpb):
    w, cb = GRID_W, NA_WIN_COLS
    cols = np.arange(w)
    rel = np.clip(cols[None, :] - cols[:, None], 1 - NA_WIN_COLS, NA_WIN_COLS - 1)
    by_col = rpb[:, :, rel + (NA_WIN_COLS - 1)]
    idx = np.arange(NA_WIN_ROWS)[:, None] + np.arange(NA_WIN_ROWS)[None, :]
    t = by_col[:, idx]
    t = t.reshape(NA_HEADS, NA_WIN_ROWS, NA_WIN_ROWS, w, w // cb, cb)
    t = t.transpose(0, 1, 3, 4, 2, 5)
    return t.reshape(NA_HEADS // 2, 2, NA_WIN_ROWS, w, NA_WIN_ROWS * w)


def _na_col_mask():
    w, cb = GRID_W, NA_WIN_COLS
    m = _na_col_valid().reshape(w, w // cb, 1, cb)
    m = np.broadcast_to(m, (w, w // cb, NA_WIN_ROWS, cb))
    return jnp.asarray(m.reshape(w, NA_WIN_ROWS * w).astype(np.int32))


def _neighbourhood_attention(qkv3, rpb):
    bsz, seq, d3 = qkv3.shape
    d = d3 // 3
    rows = seq // GRID_W
    assert rows >= NA_WIN_ROWS and rows % (2 * NA_ROWS_PER_STEP) == 0 and d == NA_HEADS * NA_HEAD_DIM
    pair = 2 * NA_HEAD_DIM
    n_pairs = d // pair
    keys = NA_WIN_ROWS * GRID_W
    assert NA_WIN_ROWS * NA_WIN_COLS == V7X_LANES and GRID_W % NA_WIN_COLS == 0
    blk = lambda off: pl.BlockSpec((1, seq, pair), lambda hp, b: (b, 0, off + hp))
    stage = pltpu.VMEM((NA_ROWS_PER_STEP, 2 * GRID_W, keys), F32)
    windows = pltpu.VMEM((GRID_W // NA_WIN_COLS, rows * NA_WIN_COLS, pair), BF16)
    queries = pltpu.VMEM((2 * seq, pair), BF16)
    return pl.pallas_call(
        functools.partial(_na_body, rows=rows),
        grid=(n_pairs, bsz),
        in_specs=[blk(0), blk(n_pairs), blk(2 * n_pairs),
                  pl.BlockSpec((1, 2, NA_WIN_ROWS, GRID_W, keys), lambda hp, b: (hp, 0, 0, 0, 0)),
                  _resident((GRID_W, keys))],
        out_specs=pl.BlockSpec((1, seq, pair), lambda hp, b: (b, 0, hp)),
        out_shape=jax.ShapeDtypeStruct((bsz, seq, d), BF16),
        scratch_shapes=[queries] + [windows] * 2 + [stage] * 2,
        compiler_params=_params(2),
        name="neighbourhood_attention",
    )(qkv3, qkv3, qkv3, _na_bias_tables(rpb), _na_col_mask())


def kernel(x, mix_norm, ffn_norm, ffn_w_gate, ffn_w_up, ffn_w_down, ab_w_in, conv_w, conv_b, conv_ln_g, conv_ln_b, ssm_lambda_re, ssm_lambda_im, ssm_log_step, ssm_b_re, ssm_b_im, ssm_c_re, ssm_c_im, ssm_d, ssm_glu_w, ssm_glu_b, ab_w_out, na_w_qkv, na_q_norm, na_k_norm, na_rpb, na_w_out):
    bsz, seq, d = x.shape
    m = bsz * seq
    chunk = S5_CHUNK
    assert m % ROW_TILE == 0 and seq % chunk == 0
    n_chunks = seq // chunk
    rows_c = bsz * n_chunks
    depth = mix_norm.shape[0]
    bf = lambda w: w.astype(BF16)
    xs = x.reshape(m, d)
    for layer in range(depth):
        i = layer // 2
        ffn = (ffn_norm[layer], bf(ffn_w_gate[layer]), bf(ffn_w_up[layer]), bf(ffn_w_down[layer]))
        if layer % 2 == 0:
            xc = xs.reshape(rows_c, chunk, d)
            z, ut = _ssm_in_proj(xc, mix_norm[layer], bf(ab_w_in[i]))
            ya = _conv_module(z.reshape(bsz, seq, -1), conv_w[i], conv_b[i], conv_ln_g[i],
                              conv_ln_b[i])
            yt = _s5_scan(ut, ssm_lambda_re[i], ssm_lambda_im[i], ssm_log_step[i], ssm_b_re[i],
                          ssm_b_im[i], ssm_c_re[i], ssm_c_im[i], n_chunks)
            ys = _s5_unlayout(yt)
            xs = _ssm_ffn(xs, ya.reshape(m, -1), ys.reshape(m, -1), z.reshape(m, -1), ssm_d[i],
                          bf(ssm_glu_w[i]), ssm_glu_b[i], bf(ab_w_out[i]), *ffn)
        else:
            qkv = _qkv_proj(xs, mix_norm[layer], bf(na_w_qkv[i]), na_q_norm[i], na_k_norm[i])
            attn = _neighbourhood_attention(qkv.reshape(bsz, seq, -1), na_rpb[i])
            xs = _attn_ffn(xs, attn.reshape(m, d), bf(na_w_out[i]), *ffn)
    return xs.reshape(bsz, seq, d)
```

```python
import functools

import numpy as np
import jax
import jax.numpy as jnp
from jax import lax
from jax.experimental import pallas as pl
from jax.experimental.pallas import tpu as pltpu

F32 = jnp.float32
BF16 = jnp.bfloat16

GRID_W = 64
CONV_DIM = 512
CONV_WIDTH = 31
SSM_DIM = 512
SSM_GROUP = 16
SSM_GROUPS = SSM_DIM // SSM_GROUP
SSM_STATE = 64
NA_HEADS = 16
NA_HEAD_DIM = 64
NA_WIN_ROWS = 8
NA_WIN_COLS = 16
NORM_EPS = 1e-6
LN_EPS = 1e-5
MASK_VALUE = -1e30

S5_CHUNK = 64
S5_STEP_BLOCK = 8
S5_CHUNK_BLOCK = 128
V7X_LANES = 128
V7X_SUBLANES = 8
V7X_VMEM_LIMIT = 56 * 1024 * 1024
ROW_TILE = 512
CONV_TILE = 256
CONV_PAD = 16
NA_ROWS_PER_STEP = 4


def _params(n_axes):
    return pltpu.CompilerParams(dimension_semantics=("arbitrary",) * n_axes,
                                vmem_limit_bytes=V7X_VMEM_LIMIT)


def _resident(shape):
    return pl.BlockSpec(shape, lambda *_: (0,) * len(shape), pipeline_mode=pl.Buffered(1))


def _rms_norm(x, gain):
    ms = jnp.mean(x * x, axis=-1, keepdims=True)
    return x * lax.rsqrt(ms + NORM_EPS) * gain


def _dot(a, b):
    return jnp.dot(a, b, preferred_element_type=F32)


def _dot_nt(a, b):
    return lax.dot_general(a, b, (((1,), (1,)), ((), ())), preferred_element_type=F32)


def _qkv_proj_body(x_ref, g_ref, w_ref, qg_ref, kg_ref, o_ref):
    h = _rms_norm(x_ref[...], g_ref[...]).astype(BF16)
    d = x_ref.shape[1]
    pair = 2 * NA_HEAD_DIM
    head0 = lax.broadcasted_iota(jnp.int32, (1, pair), 1) < NA_HEAD_DIM
    for part, gain_ref in enumerate((qg_ref, kg_ref)):
        t = _dot(h, w_ref[:, part * d:(part + 1) * d])
        for l in range(d // pair):
            cols = slice(l * pair, (l + 1) * pair)
            tl = t[:, cols]
            sq = tl * tl
            both = jnp.sum(sq, axis=-1, keepdims=True)
            first = jnp.sum(jnp.where(head0, sq, 0.0), axis=-1, keepdims=True)
            ms = jnp.where(head0, first, both - first) / NA_HEAD_DIM
            o_ref[:, part * d + l * pair:part * d + (l + 1) * pair] = (
                tl * lax.rsqrt(ms + NORM_EPS) * gain_ref[:, cols]).astype(o_ref.dtype)
    o_ref[:, 2 * d:] = _dot(h, w_ref[:, 2 * d:]).astype(o_ref.dtype)


def _qkv_proj(x, gain, w, q_gain, k_gain):
    m, d = x.shape
    n = w.shape[1]
    heads = d // NA_HEAD_DIM
    qg = jnp.tile(q_gain, heads).reshape(1, d) * (NA_HEAD_DIM ** -0.5)
    kg = jnp.tile(k_gain, heads).reshape(1, d)
    return pl.pallas_call(
        _qkv_proj_body,
        grid=(m // ROW_TILE,),
        in_specs=[pl.BlockSpec((ROW_TILE, d), lambda i: (i, 0)),
                  _resident((1, d)), _resident((d, n)), _resident((1, d)), _resident((1, d))],
        out_specs=pl.BlockSpec((ROW_TILE, n), lambda i: (i, 0)),
        out_shape=jax.ShapeDtypeStruct((m, n), BF16),
        compiler_params=_params(1),
        name="qkv_proj",
    )(x, gain.reshape(1, d), w, qg, kg)


def _ffn_tail(x1, fg_ref, wg_ref, wu_ref, wd_ref):
    h = _rms_norm(x1, fg_ref[...]).astype(BF16)
    g = _dot(h, wg_ref[...])
    u = _dot(h, wu_ref[...])
    a = (g * jax.nn.sigmoid(g) * u).astype(BF16)
    return x1 + _dot(a, wd_ref[...])


def _attn_ffn_body(x_ref, a_ref, wo_ref, fg_ref, wg_ref, wu_ref, wd_ref, o_ref):
    x1 = x_ref[...] + _dot(a_ref[...], wo_ref[...])
    o_ref[...] = _ffn_tail(x1, fg_ref, wg_ref, wu_ref, wd_ref)


def _ssm_ffn_body(x_ref, ya_ref, ys_ref, u_ref, d_ref, gw_ref, gb_ref, wo_ref,
                  fg_ref, wg_ref, wu_ref, wd_ref, o_ref):
    y = ys_ref[...] + d_ref[...] * u_ref[...]
    y = jax.nn.gelu(y, approximate=True)
    gate = jax.nn.sigmoid(_dot(y.astype(BF16), gw_ref[...]) + gb_ref[...])
    yb = (y * gate).astype(BF16)
    x1 = (x_ref[...] + _dot(ya_ref[...], wo_ref[:CONV_DIM, :]) + _dot(yb, wo_ref[CONV_DIM:, :]))
    o_ref[...] = _ffn_tail(x1, fg_ref, wg_ref, wu_ref, wd_ref)


def _row_spec(width, col_block=0):
    return pl.BlockSpec((ROW_TILE, width), lambda i: (i, col_block))


def _attn_ffn(x, attn, w_out, ffn_gain, w_gate, w_up, w_down):
    m, d = x.shape
    f = w_gate.shape[1]
    return pl.pallas_call(
        _attn_ffn_body,
        grid=(m // ROW_TILE,),
        in_specs=[_row_spec(d), _row_spec(d), _resident((d, d)), _resident((1, d)),
                  _resident((d, f)), _resident((d, f)), _resident((f, d))],
        out_specs=_row_spec(d),
        out_shape=jax.ShapeDtypeStruct((m, d), F32),
        compiler_params=_params(1),
        name="attn_out_ffn",
    )(x, attn, w_out, ffn_gain.reshape(1, d), w_gate, w_up, w_down)


def _ssm_ffn(x, ya, ys, z, d_skip, glu_w, glu_b, w_out, ffn_gain, w_gate, w_up, w_down):
    m, d = x.shape
    f = w_gate.shape[1]
    c = SSM_DIM
    return pl.pallas_call(
        _ssm_ffn_body,
        grid=(m // ROW_TILE,),
        in_specs=[_row_spec(d), _row_spec(CONV_DIM), _row_spec(c),
                  _row_spec(c, col_block=CONV_DIM // c),
                  _resident((1, c)), _resident((c, c)), _resident((1, c)),
                  _resident((CONV_DIM + c, d)), _resident((1, d)),
                  _resident((d, f)), _resident((d, f)), _resident((f, d))],
        out_specs=_row_spec(d),
        out_shape=jax.ShapeDtypeStruct((m, d), F32),
        compiler_params=_params(1),
        name="ssm_out_ffn",
    )(x, ya, ys, z, d_skip.reshape(1, c), glu_w, glu_b.reshape(1, c), w_out,
      ffn_gain.reshape(1, d), w_gate, w_up, w_down)


def _token_block_spec(width):
    return pl.BlockSpec((S5_CHUNK_BLOCK, S5_STEP_BLOCK, width), lambda a, cb: (cb, a, 0))


def _slab_block_spec():
    return pl.BlockSpec((SSM_GROUPS, S5_STEP_BLOCK, SSM_GROUP, S5_CHUNK_BLOCK),
                        lambda a, cb: (0, a, 0, cb))


def _ssm_in_proj_body(x_ref, g_ref, w_ref, z_ref, ut_ref, u_scr):
    cb, st, d = x_ref.shape
    h = _rms_norm(x_ref[...].reshape(cb * st, d), g_ref[...]).astype(BF16)
    z = _dot(h, w_ref[...])
    glu = z[:, :CONV_DIM] * jax.nn.sigmoid(z[:, CONV_DIM:2 * CONV_DIM])
    z_ref[...] = jnp.concatenate([glu, z[:, 2 * CONV_DIM:]], axis=1).reshape(cb, st, -1)
    lanes = V7X_LANES
    groups = lanes // SSM_GROUP
    for l in range(SSM_DIM // lanes):
        u_scr[l] = z[:, 2 * CONV_DIM + l * lanes:2 * CONV_DIM + (l + 1) * lanes]
        for t in range(st):
            ut = u_scr[l, pl.ds(t, cb, stride=st), :].T
            ut_ref[l * groups:(l + 1) * groups, t] = (
                ut.reshape(groups, SSM_GROUP, cb).astype(ut_ref.dtype))


def _ssm_in_proj(xc, gain, w):
    rows, chunk, d = xc.shape
    n = w.shape[1]
    n_out = CONV_DIM + SSM_DIM
    assert rows % S5_CHUNK_BLOCK == 0 and chunk % S5_STEP_BLOCK == 0
    return pl.pallas_call(
        _ssm_in_proj_body,
        grid=(chunk // S5_STEP_BLOCK, rows // S5_CHUNK_BLOCK),
        in_specs=[_token_block_spec(d), _resident((1, d)), _resident((d, n))],
        out_specs=[_token_block_spec(n_out), _slab_block_spec()],
        out_shape=[jax.ShapeDtypeStruct((rows, chunk, n_out), F32),
                   jax.ShapeDtypeStruct((SSM_GROUPS, chunk, SSM_GROUP, rows), BF16)],
        scratch_shapes=[pltpu.VMEM((SSM_DIM // V7X_LANES, S5_CHUNK_BLOCK * S5_STEP_BLOCK, V7X_LANES),
                                   F32)],
        compiler_params=_params(2),
        name="ssm_in_proj",
    )(xc, gain.reshape(1, d), w)


def _s5_unlayout_body(yt_ref, ys_ref, y_scr):
    g, st, p, cb = yt_ref.shape
    lanes = V7X_LANES
    groups = lanes // p
    for l in range(g * p // lanes):
        for t in range(st):
            y = yt_ref[l * groups:(l + 1) * groups, t].astype(F32).reshape(lanes, cb).T
            y_scr[l, pl.ds(t, cb, stride=st), :] = y
        ys_ref[:, :, l * lanes:(l + 1) * lanes] = y_scr[l].reshape(cb, st, lanes)


def _s5_unlayout(yt):
    g, chunk, p, rows = yt.shape
    return pl.pallas_call(
        _s5_unlayout_body,
        grid=(chunk // S5_STEP_BLOCK, rows // S5_CHUNK_BLOCK),
        in_specs=[_slab_block_spec()],
        out_specs=_token_block_spec(g * p),
        out_shape=jax.ShapeDtypeStruct((rows, chunk, g * p), F32),
        scratch_shapes=[pltpu.VMEM((g * p // V7X_LANES, S5_CHUNK_BLOCK * S5_STEP_BLOCK, V7X_LANES),
                                   F32)],
        compiler_params=_params(2),
        name="s5_unlayout",
    )(yt)


def _conv_body(a_ref, w_ref, cb_ref, lg_ref, lb_ref, o_ref, apad_ref, *, seq):
    c = CONV_DIM
    apad_ref[0:CONV_PAD, :] = jnp.zeros((CONV_PAD, c), F32)
    apad_ref[CONV_PAD + seq:, :] = jnp.zeros((CONV_PAD, c), F32)

    def stage_step(i, carry):
        r0 = pl.multiple_of(i * CONV_TILE, CONV_TILE)
        apad_ref[pl.ds(CONV_PAD + r0, CONV_TILE), :] = a_ref[0, pl.ds(r0, CONV_TILE), :]
        return carry

    lax.fori_loop(0, seq // CONV_TILE, stage_step, 0)

    first = CONV_PAD - CONV_WIDTH // 2
    sub = V7X_SUBLANES

    def conv_step(i, carry):
        r0 = pl.multiple_of(i * CONV_TILE, CONV_TILE)
        win = apad_ref[pl.ds(r0, CONV_TILE + 2 * CONV_PAD), :]
        acc = cb_ref[...]
        for r in range(sub):
            part = None
            for k in range(r, CONV_WIDTH, sub):
                term = w_ref[k:k + 1, :] * win[k - r:k - r + CONV_TILE + sub]
                part = term if part is None else part + term
            acc = acc + part[first + r:first + r + CONV_TILE]
        mu = jnp.mean(acc, axis=-1, keepdims=True)
        xc = acc - mu
        var = jnp.mean(xc * xc, axis=-1, keepdims=True)
        y = xc * lax.rsqrt(var + LN_EPS) * lg_ref[...] + lb_ref[...]
        o_ref[0, pl.ds(r0, CONV_TILE), :] = (y * jax.nn.sigmoid(y)).astype(o_ref.dtype)
        return carry

    lax.fori_loop(0, seq // CONV_TILE, conv_step, 0)


def _conv_module(z3, conv_w, conv_b, ln_g, ln_b):
    bsz, seq, _ = z3.shape
    c = CONV_DIM
    return pl.pallas_call(
        functools.partial(_conv_body, seq=seq),
        grid=(bsz,),
        in_specs=[pl.BlockSpec((1, seq, c), lambda b: (b, 0, 0)),
                  _resident((CONV_WIDTH, c)), _resident((1, c)), _resident((1, c)),
                  _resident((1, c))],
        out_specs=pl.BlockSpec((1, seq, c), lambda b: (b, 0, 0)),
        out_shape=jax.ShapeDtypeStruct((bsz, seq, c), BF16),
        scratch_shapes=[pltpu.VMEM((seq + 2 * CONV_PAD, c), F32)],
        compiler_params=_params(1),
        name="conformer_conv",
    )(z3, conv_w, conv_b.reshape(1, c), ln_g.reshape(1, c), ln_b.reshape(1, c))


def _s5_body(xt_ref, par_ref, btr_ref, bti_ref, cr_ref, ci_ref, yt_ref, t_ref, w8_ref,
             *, chunk, n_chunks):
    n = SSM_STATE
    n2 = 2 * n
    p = SSM_GROUP
    r = p * chunk
    fwd = lax.broadcasted_iota(jnp.int32, (1, n2), 1) < n
    step = lax.broadcasted_iota(jnp.int32, (chunk, 1), 0).astype(F32)

    def split(a, terms):
        out = []
        for _ in range(terms):
            hi = a.astype(BF16)
            out.append(hi)
            a = a - hi.astype(F32)
        return out

    def expand(a):
        return jnp.concatenate(
            [jnp.broadcast_to(a[j:j + 1, :], (p, a.shape[1])) for j in range(chunk)], axis=0)

    def tile_rows(a):
        return jnp.concatenate([a] * chunk, axis=0)

    lr = par_ref[0, 0:1, :]
    li = par_ref[0, 1:2, :]
    dt = jnp.exp(par_ref[0, 2:3, :])
    lr_dt = lr * dt
    li_dt = li * dt

    def cpow(e):
        mag = jnp.exp(lr_dt * e)
        ang = li_dt * e
        return mag * jnp.cos(ang), mag * jnp.sin(ang)

    ar, ai = cpow(1.0)
    den = lr * lr + li * li
    pr = ar - 1.0
    coef_re = (pr * lr + ai * li) / den
    coef_im = (ai * lr - pr * li) / den
    btr = btr_ref[0]
    bti = bti_ref[0]
    bbr = tile_rows(coef_re * btr - coef_im * bti)
    bbi = tile_rows(coef_re * bti + coef_im * btr)
    ctr = tile_rows(cr_ref[0])
    cti = tile_rows(ci_ref[0])
    e_m = jnp.where(fwd, (chunk - 1.0) - step, step)
    e_q = jnp.where(fwd, step + 1.0, chunk - step)
    powers = expand(jnp.concatenate(cpow(e_m) + cpow(e_q), axis=1))
    pmr, pmi, pqr, pqi = (powers[:, i * n2:(i + 1) * n2] for i in range(4))
    mr = bbr * pmr - bbi * pmi
    mi = bbr * pmi + bbi * pmr
    qr = ctr * pqr - cti * pqi
    qi = ctr * pqi + cti * pqr

    def strip_rows(v):
        zero = jnp.zeros_like(v)
        f = jnp.where(fwd, v, zero)
        b = jnp.where(fwd, zero, v)
        return (jnp.concatenate([f, zero], axis=0)
                + jnp.concatenate([zero[:r - p], b, zero[:p]], axis=0))

    def dot_nt_split(a, b_terms):
        a_hi, a_lo = split(a, 2)
        b_hi, b_lo = b_terms
        return _dot_nt(a_hi, b_hi) + _dot_nt(a_hi, b_lo) + _dot_nt(a_lo, b_hi)

    mr_terms = split(mr, 2)
    mi_terms = split(mi, 2)
    wc = (dot_nt_split(cr_ref[0], [strip_rows(v) for v in mr_terms])
          - dot_nt_split(ci_ref[0], [strip_rows(v) for v in mi_terms]))

    width = 2 * r - V7X_LANES
    for s in range(V7X_LANES // p):
        w8_ref[s] = wc[:, p * s:p * s + width]
    for t in range(chunk):
        a, s = divmod(p * (chunk - 1 - t), V7X_LANES)
        t_ref[p * t:p * (t + 1), :] = w8_ref[s // p, :, V7X_LANES * a:V7X_LANES * a + r].astype(BF16)

    cols = xt_ref.shape[-1]
    xt = xt_ref[0].reshape(r, cols)
    y = _dot(t_ref[...], xt)

    e = _dot(jnp.concatenate([mr, mi], axis=1).T.astype(BF16), xt)
    lvl = lax.broadcasted_iota(jnp.int32, (V7X_SUBLANES, 1), 0)
    lre, lim = cpow((chunk * jnp.left_shift(1, lvl)).astype(F32))
    pcol = jnp.concatenate([lre, lim, jnp.zeros((n2 - 2 * V7X_SUBLANES, n2), F32)], axis=0).T
    jl = lax.broadcasted_iota(jnp.int32, (1, cols), 1) % n_chunks

    prev = {}
    for d in range(2):
        rows_d = slice(d * n, (d + 1) * n)
        sr, si = e[rows_d], e[n2 + d * n:n2 + (d + 1) * n]
        shifted = lambda v, s: pltpu.roll(v, s if d == 0 else cols - s, axis=1)
        k, s = 0, 1
        while s < n_chunks:
            keep = (jl >= s) if d == 0 else (jl < n_chunks - s)
            a_r = pcol[rows_d, k:k + 1]
            a_i = pcol[rows_d, V7X_SUBLANES + k:V7X_SUBLANES + k + 1]
            rr, ri = shifted(sr, s), shifted(si, s)
            sr, si = (sr + jnp.where(keep, a_r * rr - a_i * ri, 0.0),
                      si + jnp.where(keep, a_r * ri + a_i * rr, 0.0))
            k, s = k + 1, 2 * s
        keep = (jl >= 1) if d == 0 else (jl < n_chunks - 1)
        prev[d] = (jnp.where(keep, shifted(sr, 1), 0.0), jnp.where(keep, shifted(si, 1), 0.0))

    s_prev = jnp.concatenate([prev[0][0], prev[1][0], prev[0][1], prev[1][1]], axis=0)
    q_cat = jnp.concatenate([qr, -qi], axis=1)
    y = y + _dot(q_cat.astype(BF16), s_prev.astype(BF16))
    yt_ref[0] = y.reshape(chunk, p, cols).astype(yt_ref.dtype)


def _s5_scan(ut, lam_re, lam_im, log_step, b_re, b_im, c_re, c_im, n_chunks):
    g, chunk, p, cols = ut.shape
    n2 = 2 * SSM_STATE
    assert cols % V7X_LANES == 0 and n_chunks <= 2 ** V7X_SUBLANES
    r = p * chunk
    both = lambda a: a.transpose(1, 2, 0, 3).reshape(g, -1, n2)
    par = both(jnp.stack([lam_re, lam_im, jnp.broadcast_to(log_step[..., None], lam_re.shape)],
                         axis=2))
    bt_re = both(b_re.transpose(0, 1, 3, 2))
    bt_im = both(b_im.transpose(0, 1, 3, 2))
    grp = lambda *tail: pl.BlockSpec((1,) + tail, lambda i: (i,) + (0,) * len(tail))
    return pl.pallas_call(
        functools.partial(_s5_body, chunk=chunk, n_chunks=n_chunks),
        grid=(g,),
        in_specs=[grp(chunk, p, cols), grp(3, n2), grp(p, n2), grp(p, n2), grp(p, n2), grp(p, n2)],
        out_specs=grp(chunk, p, cols),
        out_shape=jax.ShapeDtypeStruct(ut.shape, BF16),
        scratch_shapes=[pltpu.VMEM((r, r), BF16),
                        pltpu.VMEM((V7X_LANES // p, p, 2 * r - V7X_LANES), F32)],
        compiler_params=_params(1),
        name="s5_chunked",
    )(ut, par, bt_re, bt_im, both(c_re), both(c_im))


def _na_body(q_ref, k_ref, v_ref, tab_ref, mask_ref, o_ref,
             qq_ref, kw_ref, vw_ref, sa_ref, sb_ref, *, rows):
    w = GRID_W
    pair = 2 * NA_HEAD_DIM
    head0 = lax.broadcasted_iota(jnp.int32, (1, pair), 1) < NA_HEAD_DIM
    qn = q_ref[0]
    qh = (jnp.where(head0, qn, jnp.zeros_like(qn)), jnp.where(head0, jnp.zeros_like(qn), qn))
    kn = k_ref[0]

    cb = NA_WIN_COLS
    n_blocks = w // cb
    lanes = NA_WIN_ROWS * cb
    live = _na_live_tiles()
    vv = v_ref[0]
    for row in range(rows):
        for t in range(n_blocks):
            src = slice(row * w + t * cb, row * w + (t + 1) * cb)
            kw_ref[t, row * cb:(row + 1) * cb, :] = kn[src]
            vw_ref[t, row * cb:(row + 1) * cb, :] = vv[src]
        stacked = jnp.concatenate([t[row * w:(row + 1) * w] for t in qh], axis=0)
        qq_ref[2 * row * w:2 * (row + 1) * w, :] = stacked

    n_steps = rows // NA_ROWS_PER_STEP
    sub = V7X_SUBLANES

    def offsets(step, j):
        row = step * NA_ROWS_PER_STEP + j
        start = jnp.clip(row - NA_WIN_ROWS // 2, 0, rows - NA_WIN_ROWS)
        return (row, pl.multiple_of(start * cb, cb), start - row + (NA_WIN_ROWS - 1))

    def window(ref, k_off):
        return jnp.concatenate([ref[t, pl.ds(k_off, lanes), :] for t in range(n_blocks)], axis=0)

    def score_row(step, j, stage_ref):
        row, k_off, _ = offsets(step, j)
        q_off = pl.multiple_of(row * (2 * w), 2 * w)
        stage_ref[j] = _dot_nt(qq_ref[pl.ds(q_off, 2 * w), :], window(kw_ref, k_off))

    def value_row(step, j, stage_ref):
        row, k_off, bias_row = offsets(step, j)
        p_rows, l_rows = [], []
        for h in range(2):
            for a in range(w // sub):
                qs = slice(a * sub, (a + 1) * sub)
                tiles = {}
                for t in live[a]:
                    ks = slice(t * lanes, (t + 1) * lanes)
                    s = stage_ref[j, h * w + a * sub:h * w + (a + 1) * sub, ks]
                    s = s + tab_ref[0, h, bias_row, qs, ks]
                    tiles[t] = jnp.where(mask_ref[qs, ks] != 0, s, MASK_VALUE)
                m = functools.reduce(jnp.maximum, tiles.values())
                m = jnp.maximum(jnp.max(m, axis=-1, keepdims=True), MASK_VALUE)
                tiles = {t: jnp.exp(s - m) for t, s in tiles.items()}
                masked = jnp.exp(MASK_VALUE - m)
                n_masked = (n_blocks - len(tiles)) * lanes
                l_rows.append(jnp.sum(functools.reduce(jnp.add, tiles.values()), axis=-1,
                                      keepdims=True) + n_masked * masked)
                fill = jnp.broadcast_to(masked, (sub, lanes))
                p_rows.append(jnp.concatenate([tiles.get(t, fill) for t in range(n_blocks)], axis=1))
        p = jnp.concatenate(p_rows, axis=0).astype(BF16)
        o = _dot(p, window(vw_ref, k_off)) / jnp.concatenate(l_rows, axis=0)
        q_off = pl.multiple_of(row * w, w)
        o_ref[0, pl.ds(q_off, w), :] = jnp.where(head0, o[:w], o[w:]).astype(o_ref.dtype)

    for j in range(NA_ROWS_PER_STEP):
        score_row(0, j, sa_ref)

    def step_pair(i, carry):
        even = 2 * i
        for j in range(NA_ROWS_PER_STEP):
            score_row(even + 1, j, sb_ref)
            value_row(even, j, sa_ref)
        nxt = jnp.minimum(even + 2, n_steps - 1)
        for j in range(NA_ROWS_PER_STEP):
            score_row(nxt, j, sa_ref)
            value_row(even + 1, j, sb_ref)
        return carry

    lax.fori_loop(0, n_steps // 2, step_pair, 0)


def _na_col_start():
    cols = np.arange(GRID_W)
    return np.clip(cols - NA_WIN_COLS // 2, 0, GRID_W - NA_WIN_COLS)


def _na_col_valid():
    cols = np.arange(GRID_W)
    cs = _na_col_start()[:, None]
    return (cols[None, :] >= cs) & (cols[None, :] < cs + NA_WIN_COLS)


def _na_live_tiles():
    valid = _na_col_valid().reshape(GRID_W // V7X_SUBLANES, V7X_SUBLANES,
                                    GRID_W // NA_WIN_COLS, NA_WIN_COLS)
    return [list(np.nonzero(tile.any(axis=(0, 2)))[0]) for tile in valid]


def _na_bias_tables(rpb):
    w, cb = GRID_W, NA_WIN_COLS
    cols = np.arange(w)
    rel = np.clip(cols[None, :] - cols[:, None], 1 - NA_WIN_COLS, NA_WIN_COLS - 1)
    by_col = rpb[:, :, rel + (NA_WIN_COLS - 1)]
    idx = np.arange(NA_WIN_ROWS)[:, None] + np.arange(NA_WIN_ROWS)[None, :]
    t = by_col[:, idx]
    t = t.reshape(NA_HEADS, NA_WIN_ROWS, NA_WIN_ROWS, w, w // cb, cb)
    t = t.transpose(0, 1, 3, 4, 2, 5)
    return t.reshape(NA_HEADS // 2, 2, NA_WIN_ROWS, w, NA_WIN_ROWS * w)


def _na_col_mask():
    w, cb = GRID_W, NA_WIN_COLS
    m = _na_col_valid().reshape(w, w // cb, 1, cb)
    m = np.broadcast_to(m, (w, w // cb, NA_WIN_ROWS, cb))
    return jnp.asarray(m.reshape(w, NA_WIN_ROWS * w).astype(np.int32))


def _neighbourhood_attention(qkv3, rpb):
    bsz, seq, d3 = qkv3.shape
    d = d3 // 3
    rows = seq // GRID_W
    assert rows >= NA_WIN_ROWS and rows % (2 * NA_ROWS_PER_STEP) == 0 and d == NA_HEADS * NA_HEAD_DIM
    pair = 2 * NA_HEAD_DIM
    n_pairs = d // pair
    keys = NA_WIN_ROWS * GRID_W
    assert NA_WIN_ROWS * NA_WIN_COLS == V7X_LANES and GRID_W % NA_WIN_COLS == 0
    blk = lambda off: pl.BlockSpec((1, seq, pair), lambda hp, b: (b, 0, off + hp))
    stage = pltpu.VMEM((NA_ROWS_PER_STEP, 2 * GRID_W, keys), F32)
    windows = pltpu.VMEM((GRID_W // NA_WIN_COLS, rows * NA_WIN_COLS, pair), BF16)
    queries = pltpu.VMEM((2 * seq, pair), BF16)
    return pl.pallas_call(
        functools.partial(_na_body, rows=rows),
        grid=(n_pairs, bsz),
        in_specs=[blk(0), blk(n_pairs), blk(2 * n_pairs),
                  pl.BlockSpec((1, 2, NA_WIN_ROWS, GRID_W, keys), lambda hp, b: (hp, 0, 0, 0, 0)),
                  _resident((GRID_W, keys))],
        out_specs=pl.BlockSpec((1, seq, pair), lambda hp, b: (b, 0, hp)),
        out_shape=jax.ShapeDtypeStruct((bsz, seq, d), BF16),
        scratch_shapes=[queries] + [windows] * 2 + [stage] * 2,
        compiler_params=_params(2),
        name="neighbourhood_attention",
    )(qkv3, qkv3, qkv3, _na_bias_tables(rpb), _na_col_mask())


def kernel(x, mix_norm, ffn_norm, ffn_w_gate, ffn_w_up, ffn_w_down, ab_w_in, conv_w, conv_b, conv_ln_g, conv_ln_b, ssm_lambda_re, ssm_lambda_im, ssm_log_step, ssm_b_re, ssm_b_im, ssm_c_re, ssm_c_im, ssm_d, ssm_glu_w, ssm_glu_b, ab_w_out, na_w_qkv, na_q_norm, na_k_norm, na_rpb, na_w_out):
    bsz, seq, d = x.shape
    m = bsz * seq
    chunk = S5_CHUNK
    assert m % ROW_TILE == 0 and seq % chunk == 0
    n_chunks = seq // chunk
    rows_c = bsz * n_chunks
    depth = mix_norm.shape[0]
    bf = lambda w: w.astype(BF16)
    xs = x.reshape(m, d)
    for layer in range(depth):
        i = layer // 2
        ffn = (ffn_norm[layer], bf(ffn_w_gate[layer]), bf(ffn_w_up[layer]), bf(ffn_w_down[layer]))
        if layer % 2 == 0:
            xc = xs.reshape(rows_c, chunk, d)
            z, ut = _ssm_in_proj(xc, mix_norm[layer], bf(ab_w_in[i]))
            ya = _conv_module(z.reshape(bsz, seq, -1), conv_w[i], conv_b[i], conv_ln_g[i],
                              conv_ln_b[i])
            yt = _s5_scan(ut, ssm_lambda_re[i], ssm_lambda_im[i], ssm_log_step[i], ssm_b_re[i],
                          ssm_b_im[i], ssm_c_re[i], ssm_c_im[i], n_chunks)
            ys = _s5_unlayout(yt)
            xs = _ssm_ffn(xs, ya.reshape(m, -1), ys.reshape(m, -1), z.reshape(m, -1), ssm_d[i],
                          bf(ssm_glu_w[i]), ssm_glu_b[i], bf(ab_w_out[i]), *ffn)
        else:
            qkv = _qkv_proj(xs, mix_norm[layer], bf(na_w_qkv[i]), na_q_norm[i], na_k_norm[i])
            attn = _neighbourhood_attention(qkv.reshape(bsz, seq, -1), na_rpb[i])
            xs = _attn_ffn(xs, attn.reshape(m, d), bf(na_w_out[i]), *ffn)
    return xs.reshape(bsz, seq, d)
```

```python
import functools

import numpy as np
import jax
import jax.numpy as jnp
from jax import lax
from jax.experimental import pallas as pl
from jax.experimental.pallas import tpu as pltpu

F32 = jnp.float32
BF16 = jnp.bfloat16

GRID_W = 64
CONV_DIM = 512
CONV_WIDTH = 31
SSM_DIM = 512
SSM_GROUP = 16
SSM_GROUPS = SSM_DIM // SSM_GROUP
SSM_STATE = 64
NA_HEADS = 16
NA_HEAD_DIM = 64
NA_WIN_ROWS = 8
NA_WIN_COLS = 16
NORM_EPS = 1e-6
LN_EPS = 1e-5
MASK_VALUE = -1e30

S5_CHUNK = 64
S5_STEP_BLOCK = 8
S5_CHUNK_BLOCK = 128
V7X_LANES = 128
V7X_SUBLANES = 8
V7X_VMEM_LIMIT = 56 * 1024 * 1024
ROW_TILE = 512
CONV_TILE = 256
CONV_PAD = 16
NA_ROWS_PER_STEP = 4
NA_BATCH_PER_STEP = 2


def _params(n_axes):
    return pltpu.CompilerParams(dimension_semantics=("arbitrary",) * n_axes,
                                vmem_limit_bytes=V7X_VMEM_LIMIT)


def _resident(shape):
    return pl.BlockSpec(shape, lambda *_: (0,) * len(shape), pipeline_mode=pl.Buffered(1))


def _rms_norm(x, gain):
    ms = jnp.mean(x * x, axis=-1, keepdims=True)
    return x * lax.rsqrt(ms + NORM_EPS) * gain


def _dot(a, b):
    return jnp.dot(a, b, preferred_element_type=F32)


def _dot_nt(a, b):
    return lax.dot_general(a, b, (((1,), (1,)), ((), ())), preferred_element_type=F32)


def _qkv_proj_body(x_ref, g_ref, w_ref, qg_ref, kg_ref, o_ref):
    h = _rms_norm(x_ref[...], g_ref[...]).astype(BF16)
    d = x_ref.shape[1]
    pair = 2 * NA_HEAD_DIM
    head0 = lax.broadcasted_iota(jnp.int32, (1, pair), 1) < NA_HEAD_DIM
    for part, gain_ref in enumerate((qg_ref, kg_ref)):
        t = _dot(h, w_ref[:, part * d:(part + 1) * d])
        for l in range(d // pair):
            cols = slice(l * pair, (l + 1) * pair)
            tl = t[:, cols]
            sq = tl * tl
            both = jnp.sum(sq, axis=-1, keepdims=True)
            first = jnp.sum(jnp.where(head0, sq, 0.0), axis=-1, keepdims=True)
            ms = jnp.where(head0, first, both - first) / NA_HEAD_DIM
            o_ref[:, part * d + l * pair:part * d + (l + 1) * pair] = (
                tl * lax.rsqrt(ms + NORM_EPS) * gain_ref[:, cols]).astype(o_ref.dtype)
    o_ref[:, 2 * d:] = _dot(h, w_ref[:, 2 * d:]).astype(o_ref.dtype)


def _qkv_proj(x, gain, w, q_gain, k_gain):
    m, d = x.shape
    n = w.shape[1]
    heads = d // NA_HEAD_DIM
    qg = jnp.tile(q_gain, heads).reshape(1, d) * (NA_HEAD_DIM ** -0.5)
    kg = jnp.tile(k_gain, heads).reshape(1, d)
    return pl.pallas_call(
        _qkv_proj_body,
        grid=(m // ROW_TILE,),
        in_specs=[pl.BlockSpec((ROW_TILE, d), lambda i: (i, 0)),
                  _resident((1, d)), _resident((d, n)), _resident((1, d)), _resident((1, d))],
        out_specs=pl.BlockSpec((ROW_TILE, n), lambda i: (i, 0)),
        out_shape=jax.ShapeDtypeStruct((m, n), BF16),
        compiler_params=_params(1),
        name="qkv_proj",
    )(x, gain.reshape(1, d), w, qg, kg)


def _ffn_tail(x1, fg_ref, wg_ref, wu_ref, wd_ref):
    h = _rms_norm(x1, fg_ref[...]).astype(BF16)
    g = _dot(h, wg_ref[...])
    u = _dot(h, wu_ref[...])
    a = (g * jax.nn.sigmoid(g) * u).astype(BF16)
    return x1 + _dot(a, wd_ref[...])


def _attn_ffn_body(x_ref, a_ref, wo_ref, fg_ref, wg_ref, wu_ref, wd_ref, o_ref):
    x1 = x_ref[...] + _dot(a_ref[...], wo_ref[...])
    o_ref[...] = _ffn_tail(x1, fg_ref, wg_ref, wu_ref, wd_ref)


def _ssm_ffn_body(x_ref, ya_ref, ys_ref, u_ref, d_ref, gw_ref, gb_ref, wo_ref,
                  fg_ref, wg_ref, wu_ref, wd_ref, o_ref):
    y = ys_ref[...] + d_ref[...] * u_ref[...]
    y = jax.nn.gelu(y, approximate=True)
    gate = jax.nn.sigmoid(_dot(y.astype(BF16), gw_ref[...]) + gb_ref[...])
    yb = (y * gate).astype(BF16)
    x1 = (x_ref[...] + _dot(ya_ref[...], wo_ref[:CONV_DIM, :]) + _dot(yb, wo_ref[CONV_DIM:, :]))
    o_ref[...] = _ffn_tail(x1, fg_ref, wg_ref, wu_ref, wd_ref)


def _row_spec(width, col_block=0):
    return pl.BlockSpec((ROW_TILE, width), lambda i: (i, col_block))


def _attn_ffn(x, attn, w_out, ffn_gain, w_gate, w_up, w_down):
    m, d = x.shape
    f = w_gate.shape[1]
    return pl.pallas_call(
        _attn_ffn_body,
        grid=(m // ROW_TILE,),
        in_specs=[_row_spec(d), _row_spec(d), _resident((d, d)), _resident((1, d)),
                  _resident((d, f)), _resident((d, f)), _resident((f, d))],
        out_specs=_row_spec(d),
        out_shape=jax.ShapeDtypeStruct((m, d), F32),
        compiler_params=_params(1),
        name="attn_out_ffn",
    )(x, attn, w_out, ffn_gain.reshape(1, d), w_gate, w_up, w_down)


def _ssm_ffn(x, ya, ys, z, d_skip, glu_w, glu_b, w_out, ffn_gain, w_gate, w_up, w_down):
    m, d = x.shape
    f = w_gate.shape[1]
    c = SSM_DIM
    return pl.pallas_call(
        _ssm_ffn_body,
        grid=(m // ROW_TILE,),
        in_specs=[_row_spec(d), _row_spec(CONV_DIM), _row_spec(c),
                  _row_spec(c, col_block=CONV_DIM // c),
                  _resident((1, c)), _resident((c, c)), _resident((1, c)),
                  _resident((CONV_DIM + c, d)), _resident((1, d)),
                  _resident((d, f)), _resident((d, f)), _resident((f, d))],
        out_specs=_row_spec(d),
        out_shape=jax.ShapeDtypeStruct((m, d), F32),
        compiler_params=_params(1),
        name="ssm_out_ffn",
    )(x, ya, ys, z, d_skip.reshape(1, c), glu_w, glu_b.reshape(1, c), w_out,
      ffn_gain.reshape(1, d), w_gate, w_up, w_down)


def _token_block_spec(width):
    return pl.BlockSpec((S5_CHUNK_BLOCK, S5_STEP_BLOCK, width), lambda a, cb: (cb, a, 0))


def _slab_block_spec():
    return pl.BlockSpec((SSM_GROUPS, S5_STEP_BLOCK, SSM_GROUP, S5_CHUNK_BLOCK),
                        lambda a, cb: (0, a, 0, cb))


def _ssm_in_proj_body(x_ref, g_ref, w_ref, z_ref, ut_ref, u_scr):
    cb, st, d = x_ref.shape
    h = _rms_norm(x_ref[...].reshape(cb * st, d), g_ref[...]).astype(BF16)
    z = _dot(h, w_ref[...])
    glu = z[:, :CONV_DIM] * jax.nn.sigmoid(z[:, CONV_DIM:2 * CONV_DIM])
    z_ref[...] = jnp.concatenate([glu, z[:, 2 * CONV_DIM:]], axis=1).reshape(cb, st, -1)
    lanes = V7X_LANES
    groups = lanes // SSM_GROUP
    for l in range(SSM_DIM // lanes):
        u_scr[l] = z[:, 2 * CONV_DIM + l * lanes:2 * CONV_DIM + (l + 1) * lanes]
        for t in range(st):
            ut = u_scr[l, pl.ds(t, cb, stride=st), :].T
            ut_ref[l * groups:(l + 1) * groups, t] = (
                ut.reshape(groups, SSM_GROUP, cb).astype(ut_ref.dtype))


def _ssm_in_proj(xc, gain, w):
    rows, chunk, d = xc.shape
    n = w.shape[1]
    n_out = CONV_DIM + SSM_DIM
    assert rows % S5_CHUNK_BLOCK == 0 and chunk % S5_STEP_BLOCK == 0
    return pl.pallas_call(
        _ssm_in_proj_body,
        grid=(chunk // S5_STEP_BLOCK, rows // S5_CHUNK_BLOCK),
        in_specs=[_token_block_spec(d), _resident((1, d)), _resident((d, n))],
        out_specs=[_token_block_spec(n_out), _slab_block_spec()],
        out_shape=[jax.ShapeDtypeStruct((rows, chunk, n_out), F32),
                   jax.ShapeDtypeStruct((SSM_GROUPS, chunk, SSM_GROUP, rows), BF16)],
        scratch_shapes=[pltpu.VMEM((SSM_DIM // V7X_LANES, S5_CHUNK_BLOCK * S5_STEP_BLOCK, V7X_LANES),
                                   F32)],
        compiler_params=_params(2),
        name="ssm_in_proj",
    )(xc, gain.reshape(1, d), w)


def _s5_unlayout_body(yt_ref, ys_ref, y_scr):
    g, st, p, cb = yt_ref.shape
    lanes = V7X_LANES
    groups = lanes // p
    for l in range(g * p // lanes):
        for t in range(st):
            y = yt_ref[l * groups:(l + 1) * groups, t].astype(F32).reshape(lanes, cb).T
            y_scr[l, pl.ds(t, cb, stride=st), :] = y
        ys_ref[:, :, l * lanes:(l + 1) * lanes] = y_scr[l].reshape(cb, st, lanes)


def _s5_unlayout(yt):
    g, chunk, p, rows = yt.shape
    return pl.pallas_call(
        _s5_unlayout_body,
        grid=(chunk // S5_STEP_BLOCK, rows // S5_CHUNK_BLOCK),
        in_specs=[_slab_block_spec()],
        out_specs=_token_block_spec(g * p),
        out_shape=jax.ShapeDtypeStruct((rows, chunk, g * p), F32),
        scratch_shapes=[pltpu.VMEM((g * p // V7X_LANES, S5_CHUNK_BLOCK * S5_STEP_BLOCK, V7X_LANES),
                                   F32)],
        compiler_params=_params(2),
        name="s5_unlayout",
    )(yt)


def _conv_body(a_ref, w_ref, cb_ref, lg_ref, lb_ref, o_ref, apad_ref, *, seq):
    c = CONV_DIM
    apad_ref[0:CONV_PAD, :] = jnp.zeros((CONV_PAD, c), F32)
    apad_ref[CONV_PAD + seq:, :] = jnp.zeros((CONV_PAD, c), F32)

    def stage_step(i, carry):
        r0 = pl.multiple_of(i * CONV_TILE, CONV_TILE)
        apad_ref[pl.ds(CONV_PAD + r0, CONV_TILE), :] = a_ref[0, pl.ds(r0, CONV_TILE), :]
        return carry

    lax.fori_loop(0, seq // CONV_TILE, stage_step, 0)

    first = CONV_PAD - CONV_WIDTH // 2
    sub = V7X_SUBLANES

    def conv_step(i, carry):
        r0 = pl.multiple_of(i * CONV_TILE, CONV_TILE)
        win = apad_ref[pl.ds(r0, CONV_TILE + 2 * CONV_PAD), :]
        acc = cb_ref[...]
        for r in range(sub):
            part = None
            for k in range(r, CONV_WIDTH, sub):
                term = w_ref[k:k + 1, :] * win[k - r:k - r + CONV_TILE + sub]
                part = term if part is None else part + term
            acc = acc + part[first + r:first + r + CONV_TILE]
        mu = jnp.mean(acc, axis=-1, keepdims=True)
        xc = acc - mu
        var = jnp.mean(xc * xc, axis=-1, keepdims=True)
        y = xc * lax.rsqrt(var + LN_EPS) * lg_ref[...] + lb_ref[...]
        o_ref[0, pl.ds(r0, CONV_TILE), :] = (y * jax.nn.sigmoid(y)).astype(o_ref.dtype)
        return carry

    lax.fori_loop(0, seq // CONV_TILE, conv_step, 0)


def _conv_module(z3, conv_w, conv_b, ln_g, ln_b):
    bsz, seq, _ = z3.shape
    c = CONV_DIM
    return pl.pallas_call(
        functools.partial(_conv_body, seq=seq),
        grid=(bsz,),
        in_specs=[pl.BlockSpec((1, seq, c), lambda b: (b, 0, 0)),
                  _resident((CONV_WIDTH, c)), _resident((1, c)), _resident((1, c)),
                  _resident((1, c))],
        out_specs=pl.BlockSpec((1, seq, c), lambda b: (b, 0, 0)),
        out_shape=jax.ShapeDtypeStruct((bsz, seq, c), BF16),
        scratch_shapes=[pltpu.VMEM((seq + 2 * CONV_PAD, c), F32)],
        compiler_params=_params(1),
        name="conformer_conv",
    )(z3, conv_w, conv_b.reshape(1, c), ln_g.reshape(1, c), ln_b.reshape(1, c))


def _s5_body(xt_ref, par_ref, btr_ref, bti_ref, cr_ref, ci_ref, yt_ref, t_ref, w8_ref,
             *, chunk, n_chunks):
    n = SSM_STATE
    n2 = 2 * n
    p = SSM_GROUP
    r = p * chunk
    fwd = lax.broadcasted_iota(jnp.int32, (1, n2), 1) < n
    step = lax.broadcasted_iota(jnp.int32, (chunk, 1), 0).astype(F32)

    def split(a, terms):
        out = []
        for _ in range(terms):
            hi = a.astype(BF16)
            out.append(hi)
            a = a - hi.astype(F32)
        return out

    def expand(a):
        return jnp.concatenate(
            [jnp.broadcast_to(a[j:j + 1, :], (p, a.shape[1])) for j in range(chunk)], axis=0)

    def tile_rows(a):
        return jnp.concatenate([a] * chunk, axis=0)

    lr = par_ref[0, 0:1, :]
    li = par_ref[0, 1:2, :]
    dt = jnp.exp(par_ref[0, 2:3, :])
    lr_dt = lr * dt
    li_dt = li * dt

    def cpow(e):
        mag = jnp.exp(lr_dt * e)
        ang = li_dt * e
        return mag * jnp.cos(ang), mag * jnp.sin(ang)

    ar, ai = cpow(1.0)
    den = lr * lr + li * li
    pr = ar - 1.0
    coef_re = (pr * lr + ai * li) / den
    coef_im = (ai * lr - pr * li) / den
    btr = btr_ref[0]
    bti = bti_ref[0]
    bbr = tile_rows(coef_re * btr - coef_im * bti)
    bbi = tile_rows(coef_re * bti + coef_im * btr)
    ctr = tile_rows(cr_ref[0])
    cti = tile_rows(ci_ref[0])
    e_m = jnp.where(fwd, (chunk - 1.0) - step, step)
    e_q = jnp.where(fwd, step + 1.0, chunk - step)
    powers = expand(jnp.concatenate(cpow(e_m) + cpow(e_q), axis=1))
    pmr, pmi, pqr, pqi = (powers[:, i * n2:(i + 1) * n2] for i in range(4))
    mr = bbr * pmr - bbi * pmi
    mi = bbr * pmi + bbi * pmr
    qr = ctr * pqr - cti * pqi
    qi = ctr * pqi + cti * pqr

    def strip_rows(v):
        zero = jnp.zeros_like(v)
        f = jnp.where(fwd, v, zero)
        b = jnp.where(fwd, zero, v)
        return (jnp.concatenate([f, zero], axis=0)
                + jnp.concatenate([zero[:r - p], b, zero[:p]], axis=0))

    def dot_nt_split(a, b_terms):
        a_hi, a_lo = split(a, 2)
        b_hi, b_lo = b_terms
        return _dot_nt(a_hi, b_hi) + _dot_nt(a_hi, b_lo) + _dot_nt(a_lo, b_hi)

    mr_terms = split(mr, 2)
    mi_terms = split(mi, 2)
    wc = (dot_nt_split(cr_ref[0], [strip_rows(v) for v in mr_terms])
          - dot_nt_split(ci_ref[0], [strip_rows(v) for v in mi_terms]))

    width = 2 * r - V7X_LANES
    for s in range(V7X_LANES // p):
        w8_ref[s] = wc[:, p * s:p * s + width]
    for t in range(chunk):
        a, s = divmod(p * (chunk - 1 - t), V7X_LANES)
        t_ref[p * t:p * (t + 1), :] = w8_ref[s // p, :, V7X_LANES * a:V7X_LANES * a + r].astype(BF16)

    cols = xt_ref.shape[-1]
    xt = xt_ref[0].reshape(r, cols)
    y = _dot(t_ref[...], xt)

    e = _dot(jnp.concatenate([mr, mi], axis=1).T.astype(BF16), xt)
    lvl = lax.broadcasted_iota(jnp.int32, (V7X_SUBLANES, 1), 0)
    lre, lim = cpow((chunk * jnp.left_shift(1, lvl)).astype(F32))
    pcol = jnp.concatenate([lre, lim, jnp.zeros((n2 - 2 * V7X_SUBLANES, n2), F32)], axis=0).T
    jl = lax.broadcasted_iota(jnp.int32, (1, cols), 1) % n_chunks

    prev = {}
    for d in range(2):
        rows_d = slice(d * n, (d + 1) * n)
        sr, si = e[rows_d], e[n2 + d * n:n2 + (d + 1) * n]
        shifted = lambda v, s: pltpu.roll(v, s if d == 0 else cols - s, axis=1)
        k, s = 0, 1
        while s < n_chunks:
            keep = (jl >= s) if d == 0 else (jl < n_chunks - s)
            a_r = pcol[rows_d, k:k + 1]
            a_i = pcol[rows_d, V7X_SUBLANES + k:V7X_SUBLANES + k + 1]
            rr, ri = shifted(sr, s), shifted(si, s)
            sr, si = (sr + jnp.where(keep, a_r * rr - a_i * ri, 0.0),
                      si + jnp.where(keep, a_r * ri + a_i * rr, 0.0))
            k, s = k + 1, 2 * s
        keep = (jl >= 1) if d == 0 else (jl < n_chunks - 1)
        prev[d] = (jnp.where(keep, shifted(sr, 1), 0.0), jnp.where(keep, shifted(si, 1), 0.0))

    s_prev = jnp.concatenate([prev[0][0], prev[1][0], prev[0][1], prev[1][1]], axis=0)
    q_cat = jnp.concatenate([qr, -qi], axis=1)
    y = y + _dot(q_cat.astype(BF16), s_prev.astype(BF16))
    yt_ref[0] = y.reshape(chunk, p, cols).astype(yt_ref.dtype)


def _s5_scan(ut, lam_re, lam_im, log_step, b_re, b_im, c_re, c_im, n_chunks):
    g, chunk, p, cols = ut.shape
    n2 = 2 * SSM_STATE
    assert cols % V7X_LANES == 0 and n_chunks <= 2 ** V7X_SUBLANES
    r = p * chunk
    both = lambda a: a.transpose(1, 2, 0, 3).reshape(g, -1, n2)
    par = both(jnp.stack([lam_re, lam_im, jnp.broadcast_to(log_step[..., None], lam_re.shape)],
                         axis=2))
    bt_re = both(b_re.transpose(0, 1, 3, 2))
    bt_im = both(b_im.transpose(0, 1, 3, 2))
    grp = lambda *tail: pl.BlockSpec((1,) + tail, lambda i: (i,) + (0,) * len(tail))
    return pl.pallas_call(
        functools.partial(_s5_body, chunk=chunk, n_chunks=n_chunks),
        grid=(g,),
        in_specs=[grp(chunk, p, cols), grp(3, n2), grp(p, n2), grp(p, n2), grp(p, n2), grp(p, n2)],
        out_specs=grp(chunk, p, cols),
        out_shape=jax.ShapeDtypeStruct(ut.shape, BF16),
        scratch_shapes=[pltpu.VMEM((r, r), BF16),
                        pltpu.VMEM((V7X_LANES // p, p, 2 * r - V7X_LANES), F32)],
        compiler_params=_params(1),
        name="s5_chunked",
    )(ut, par, bt_re, bt_im, both(c_re), both(c_im))


def _na_body(q_ref, *refs, rows):
    for bi in range(q_ref.shape[0]):
        _na_sequence(bi, q_ref, *refs, rows=rows)


def _na_sequence(bi, q_ref, k_ref, v_ref, tab_ref, mask_ref, o_ref,
                 qq_ref, kw_ref, vw_ref, sa_ref, sb_ref, *, rows):
    w = GRID_W
    pair = 2 * NA_HEAD_DIM
    head0 = lax.broadcasted_iota(jnp.int32, (1, pair), 1) < NA_HEAD_DIM
    qn = q_ref[bi]
    qh = (jnp.where(head0, qn, jnp.zeros_like(qn)), jnp.where(head0, jnp.zeros_like(qn), qn))
    kn = k_ref[bi]

    cb = NA_WIN_COLS
    n_blocks = w // cb
    lanes = NA_WIN_ROWS * cb
    live = _na_live_tiles()
    vv = v_ref[bi]
    for row in range(rows):
        for t in range(n_blocks):
            src = slice(row * w + t * cb, row * w + (t + 1) * cb)
            kw_ref[t, row * cb:(row + 1) * cb, :] = kn[src]
            vw_ref[t, row * cb:(row + 1) * cb, :] = vv[src]
        stacked = jnp.concatenate([t[row * w:(row + 1) * w] for t in qh], axis=0)
        qq_ref[2 * row * w:2 * (row + 1) * w, :] = stacked

    n_steps = rows // NA_ROWS_PER_STEP
    sub = V7X_SUBLANES

    def offsets(step, j):
        row = step * NA_ROWS_PER_STEP + j
        start = jnp.clip(row - NA_WIN_ROWS // 2, 0, rows - NA_WIN_ROWS)
        return (row, pl.multiple_of(start * cb, cb), start - row + (NA_WIN_ROWS - 1))

    def window(ref, k_off):
        return jnp.concatenate([ref[t, pl.ds(k_off, lanes), :] for t in range(n_blocks)], axis=0)

    def score_row(step, j, stage_ref):
        row, k_off, _ = offsets(step, j)
        q_off = pl.multiple_of(row * (2 * w), 2 * w)
        stage_ref[j] = _dot_nt(qq_ref[pl.ds(q_off, 2 * w), :], window(kw_ref, k_off))

    def value_row(step, j, stage_ref):
        row, k_off, bias_row = offsets(step, j)
        p_rows, l_rows = [], []
        for h in range(2):
            for a in range(w // sub):
                qs = slice(a * sub, (a + 1) * sub)
                tiles = {}
                for t in live[a]:
                    ks = slice(t * lanes, (t + 1) * lanes)
                    s = stage_ref[j, h * w + a * sub:h * w + (a + 1) * sub, ks]
                    s = s + tab_ref[0, h, bias_row, qs, ks]
                    tiles[t] = jnp.where(mask_ref[qs, ks] != 0, s, MASK_VALUE)
                m = functools.reduce(jnp.maximum, tiles.values())
                m = jnp.maximum(jnp.max(m, axis=-1, keepdims=True), MASK_VALUE)
                tiles = {t: jnp.exp(s - m) for t, s in tiles.items()}
                masked = jnp.exp(MASK_VALUE - m)
                n_masked = (n_blocks - len(tiles)) * lanes
                l_rows.append(jnp.sum(functools.reduce(jnp.add, tiles.values()), axis=-1,
                                      keepdims=True) + n_masked * masked)
                fill = jnp.broadcast_to(masked, (sub, lanes))
                p_rows.append(jnp.concatenate([tiles.get(t, fill) for t in range(n_blocks)], axis=1))
        p = jnp.concatenate(p_rows, axis=0).astype(BF16)
        o = _dot(p, window(vw_ref, k_off)) / jnp.concatenate(l_rows, axis=0)
        q_off = pl.multiple_of(row * w, w)
        o_ref[bi, pl.ds(q_off, w), :] = jnp.where(head0, o[:w], o[w:]).astype(o_ref.dtype)

    for j in range(NA_ROWS_PER_STEP):
        score_row(0, j, sa_ref)

    def step_pair(i, carry):
        even = 2 * i
        for j in range(NA_ROWS_PER_STEP):
            score_row(even + 1, j, sb_ref)
            value_row(even, j, sa_ref)
        nxt = jnp.minimum(even + 2, n_steps - 1)
        for j in range(NA_ROWS_PER_STEP):
            score_row(nxt, j, sa_ref)
            value_row(even + 1, j, sb_ref)
        return carry

    lax.fori_loop(0, n_steps // 2, step_pair, 0)


def _na_col_start():
    cols = np.arange(GRID_W)
    return np.clip(cols - NA_WIN_COLS // 2, 0, GRID_W - NA_WIN_COLS)


def _na_col_valid():
    cols = np.arange(GRID_W)
    cs = _na_col_start()[:, None]
    return (cols[None, :] >= cs) & (cols[None, :] < cs + NA_WIN_COLS)


def _na_live_tiles():
    valid = _na_col_valid().reshape(GRID_W // V7X_SUBLANES, V7X_SUBLANES,
                                    GRID_W // NA_WIN_COLS, NA_WIN_COLS)
    return [list(np.nonzero(tile.any(axis=(0, 2)))[0]) for tile in valid]


def _na_bias_tables(rpb):
    w, cb = GRID_W, NA_WIN_COLS
    cols = np.arange(w)
    rel = np.clip(cols[None, :] - cols[:, None], 1 - NA_WIN_COLS, NA_WIN_COLS - 1)
    by_col = rpb[:, :, rel + (NA_WIN_COLS - 1)]
    idx = np.arange(NA_WIN_ROWS)[:, None] + np.arange(NA_WIN_ROWS)[None, :]
    t = by_col[:, idx]
    t = t.reshape(NA_HEADS, NA_WIN_ROWS, NA_WIN_ROWS, w, w // cb, cb)
    t = t.transpose(0, 1, 3, 4, 2, 5)
    return t.reshape(NA_HEADS // 2, 2, NA_WIN_ROWS, w, NA_WIN_ROWS * w)


def _na_col_mask():
    w, cb = GRID_W, NA_WIN_COLS
    m = _na_col_valid().reshape(w, w // cb, 1, cb)
    m = np.broadcast_to(m, (w, w // cb, NA_WIN_ROWS, cb))
    return jnp.asarray(m.reshape(w, NA_WIN_ROWS * w).astype(np.int32))


def _neighbourhood_attention(qkv3, rpb):
    bsz, seq, d3 = qkv3.shape
    d = d3 // 3
    rows = seq // GRID_W
    assert rows >= NA_WIN_ROWS and rows % (2 * NA_ROWS_PER_STEP) == 0 and d == NA_HEADS * NA_HEAD_DIM
    pair = 2 * NA_HEAD_DIM
    n_pairs = d // pair
    keys = NA_WIN_ROWS * GRID_W
    assert NA_WIN_ROWS * NA_WIN_COLS == V7X_LANES and GRID_W % NA_WIN_COLS == 0
    nb = NA_BATCH_PER_STEP
    assert bsz % nb == 0
    blk = lambda off: pl.BlockSpec((nb, seq, pair), lambda hp, b: (b, 0, off + hp))
    stage = pltpu.VMEM((NA_ROWS_PER_STEP, 2 * GRID_W, keys), F32)
    windows = pltpu.VMEM((GRID_W // NA_WIN_COLS, rows * NA_WIN_COLS, pair), BF16)
    queries = pltpu.VMEM((2 * seq, pair), BF16)
    return pl.pallas_call(
        functools.partial(_na_body, rows=rows),
        grid=(n_pairs, bsz // nb),
        in_specs=[blk(0), blk(n_pairs), blk(2 * n_pairs),
                  pl.BlockSpec((1, 2, NA_WIN_ROWS, GRID_W, keys), lambda hp, b: (hp, 0, 0, 0, 0)),
                  _resident((GRID_W, keys))],
        out_specs=pl.BlockSpec((nb, seq, pair), lambda hp, b: (b, 0, hp)),
        out_shape=jax.ShapeDtypeStruct((bsz, seq, d), BF16),
        scratch_shapes=[queries] + [windows] * 2 + [stage] * 2,
        compiler_params=_params(2),
        name="neighbourhood_attention",
    )(qkv3, qkv3, qkv3, _na_bias_tables(rpb), _na_col_mask())


def kernel(x, mix_norm, ffn_norm, ffn_w_gate, ffn_w_up, ffn_w_down, ab_w_in, conv_w, conv_b, conv_ln_g, conv_ln_b, ssm_lambda_re, ssm_lambda_im, ssm_log_step, ssm_b_re, ssm_b_im, ssm_c_re, ssm_c_im, ssm_d, ssm_glu_w, ssm_glu_b, ab_w_out, na_w_qkv, na_q_norm, na_k_norm, na_rpb, na_w_out):
    bsz, seq, d = x.shape
    m = bsz * seq
    chunk = S5_CHUNK
    assert m % ROW_TILE == 0 and seq % chunk == 0
    n_chunks = seq // chunk
    rows_c = bsz * n_chunks
    depth = mix_norm.shape[0]
    bf = lambda w: w.astype(BF16)
    xs = x.reshape(m, d)
    for layer in range(depth):
        i = layer // 2
        ffn = (ffn_norm[layer], bf(ffn_w_gate[layer]), bf(ffn_w_up[layer]), bf(ffn_w_down[layer]))
        if layer % 2 == 0:
            xc = xs.reshape(rows_c, chunk, d)
            z, ut = _ssm_in_proj(xc, mix_norm[layer], bf(ab_w_in[i]))
            ya = _conv_module(z.reshape(bsz, seq, -1), conv_w[i], conv_b[i], conv_ln_g[i],
                              conv_ln_b[i])
            yt = _s5_scan(ut, ssm_lambda_re[i], ssm_lambda_im[i], ssm_log_step[i], ssm_b_re[i],
                          ssm_b_im[i], ssm_c_re[i], ssm_c_im[i], n_chunks)
            ys = _s5_unlayout(yt)
            xs = _ssm_ffn(xs, ya.reshape(m, -1), ys.reshape(m, -1), z.reshape(m, -1), ssm_d[i],
                          bf(ssm_glu_w[i]), ssm_glu_b[i], bf(ab_w_out[i]), *ffn)
        else:
            qkv = _qkv_proj(xs, mix_norm[layer], bf(na_w_qkv[i]), na_q_norm[i], na_k_norm[i])
            attn = _neighbourhood_attention(qkv.reshape(bsz, seq, -1), na_rpb[i])
            xs = _attn_ffn(xs, attn.reshape(m, d), bf(na_w_out[i]), *ffn)
    return xs.reshape(bsz, seq, d)
```
